```python
import jax, jax.numpy as jnp
from jax import lax
import numpy as np

D_MODEL = 1024
BATCH = 2
SEQ = 8192
DEPTH = 1
DEC_BATCH = 8
DEC_SEQ = 4096
PAST_LEN = 128

MLA_HEADS = 8
Q_LORA = 384
KV_LORA = 256
QK_NOPE = 64
QK_ROPE = 32
V_DIM = 64
QK_DIM = QK_NOPE + QK_ROPE
MLA_WIDTH = MLA_HEADS * V_DIM
ROPE_THETA = 10000.0
Q_BLOCK = 128
HG_HEADS = 4
HG_DK = 128
HG_DV = 128
HG_KW = HG_HEADS * HG_DK
HG_WIDTH = HG_HEADS * HG_DV
CHUNK = 64
N_BRANCH = 2
IN_SIZES = (Q_LORA, KV_LORA, QK_ROPE, HG_KW, HG_KW, HG_KW, HG_WIDTH, HG_WIDTH, D_MODEL, D_MODEL)
IN_COLS = sum(IN_SIZES)
IN_SPLITS = [int(v) for v in np.cumsum(IN_SIZES)[:-1]]
N_GROUPS = 4
EXPERTS_PER_GROUP = 8
N_EXPERTS = N_GROUPS * EXPERTS_PER_GROUP
TOP_K = 2
D_EXPERT = 256
EPS = 1e-6

kernel_name = 'hybrid_mla_hgrn2_hmoe_encoder'


def rmsnorm(x, g):
    xf = x.astype(jnp.float32)
    y = xf * lax.rsqrt(jnp.mean(xf * xf, axis=-1, keepdims=True) + EPS)
    return (y * g.astype(jnp.float32)).astype(x.dtype)


def rope_tables(L):
    pos = jnp.arange(L, dtype=jnp.float32)
    inv_freq = 1.0 / (ROPE_THETA ** (jnp.arange(0, QK_ROPE, 2, dtype=jnp.float32) / QK_ROPE))
    ang = pos[:, None] * inv_freq[None, :]
    return jnp.cos(ang), jnp.sin(ang)


def apply_rope(x, cos, sin):
    half = QK_ROPE // 2
    x1, x2 = x[..., :half], x[..., half:]
    cos = cos.astype(x.dtype)
    sin = sin.astype(x.dtype)
    return jnp.concatenate([x1 * cos - x2 * sin, x1 * sin + x2 * cos], axis=-1)


def mla_branch(c_q, c_kv, k_pe, q_norm, w_uq, kv_norm, w_ukv):
    B, L, _ = c_q.shape
    q = jnp.einsum('blr,rhd->blhd', rmsnorm(c_q, q_norm), w_uq)
    kv = jnp.einsum('blr,rhd->blhd', rmsnorm(c_kv, kv_norm), w_ukv)
    k_nope, v = kv[..., :QK_NOPE], kv[..., QK_NOPE:]
    cos, sin = rope_tables(L)
    scale = QK_DIM ** -0.5
    q_nope = q[..., :QK_NOPE] * scale
    q_pe = apply_rope(q[..., QK_NOPE:], cos[:, None, :], sin[:, None, :]) * scale
    k_pe = apply_rope(k_pe, cos, sin)
    nb = L // Q_BLOCK
    to_blocks = lambda t: jnp.swapaxes(t.reshape((B, nb, Q_BLOCK) + t.shape[2:]), 0, 1)

    def attend(blk):
        qn, qp = blk
        s = (jnp.einsum('bqhd,bkhd->bhqk', qn, k_nope, preferred_element_type=jnp.float32)
             + jnp.einsum('bqhr,bkr->bhqk', qp, k_pe, preferred_element_type=jnp.float32))
        p = jax.nn.softmax(s, axis=-1)
        return jnp.einsum('bhqk,bkhd->bqhd', p.astype(v.dtype), v)

    o = lax.map(attend, (to_blocks(q_nope), to_blocks(q_pe)))
    return jnp.swapaxes(o, 0, 1).reshape(B, L, MLA_WIDTH)


def chunk_gla(q, k, v, log_f):
    B, L, H, DK = q.shape
    DV = v.shape[-1]
    n = L // CHUNK
    to_chunks = lambda t: t.reshape(B, n, CHUNK, H, t.shape[-1]).transpose(1, 0, 3, 2, 4)
    causal = jnp.tril(jnp.ones((CHUNK, CHUNK), dtype=bool))[:, :, None]

    def step(S, inp):
        qi, ki, vi, gi = inp
        b = jnp.cumsum(gi, axis=-2)
        diff = b[..., :, None, :] - b[..., None, :, :]
        decay = jnp.exp(jnp.where(causal, diff, -jnp.inf))
        A = jnp.einsum('bhtsk,bhsk->bhts', qi[..., :, None, :] * decay, ki)
        o = (jnp.einsum('bhts,bhsv->bhtv', A, vi)
             + jnp.einsum('bhtk,bhkv->bhtv', qi * jnp.exp(b), S))
        b_last = b[..., -1:, :]
        S = (jnp.exp(b_last[..., 0, :])[..., :, None] * S
             + jnp.einsum('bhsk,bhsv->bhkv', ki * jnp.exp(b_last - b), vi))
        return S, o

    S0 = jnp.zeros((B, H, DK, DV), jnp.float32)
    _, o = lax.scan(step, S0, (to_chunks(q), to_chunks(k), to_chunks(v), to_chunks(log_f)))
    return o.transpose(1, 0, 3, 2, 4).reshape(B, L, H, DV)


def hgrn2_branch(hq, fz_fwd, fz_bwd, hi, hg, lb_fwd, lb_bwd, norm_g):
    B, L, _ = hq.shape
    heads = lambda t: t.astype(jnp.float32).reshape(B, L, HG_HEADS, -1)
    q = heads(hq) * (HG_DK ** -0.5)
    v = heads(hi)

    def gates(fz, lb):
        f = lb + (1.0 - lb) * jax.nn.sigmoid(fz.astype(jnp.float32))
        return heads(1.0 - f), heads(jnp.log(f))

    k_f, lf_f = gates(fz_fwd, lb_fwd)
    k_b, lf_b = gates(fz_bwd, lb_bwd)
    flip = lambda t: jnp.flip(t, axis=1)
    o = chunk_gla(q, k_f, v, lf_f) + flip(chunk_gla(flip(q), flip(k_b), flip(v), flip(lf_b)))
    o = o * lax.rsqrt(jnp.mean(o * o, axis=-1, keepdims=True) + EPS)
    o = o * norm_g.astype(jnp.float32).reshape(HG_HEADS, HG_DV)
    o = o.reshape(B, L, HG_WIDTH) * jax.nn.silu(hg.astype(jnp.float32))
    return o.astype(hq.dtype)


def hier_moe(x, w_group, b_group, w_expert, b_expert, w_gate, w_up, w_down):
    B, L, D = x.shape
    T = B * L
    xt = x.reshape(T, D)
    g_prob = jax.nn.softmax((xt @ w_group + b_group).astype(jnp.float32), axis=-1)
    g_val, g_idx = lax.top_k(g_prob, 1)
    e_logits = (xt @ w_expert + b_expert).astype(jnp.float32).reshape(T, N_GROUPS, EXPERTS_PER_GROUP)
    e_in = jnp.take_along_axis(e_logits, g_idx[:, :, None], axis=1)[:, 0]
    e_val, e_idx = lax.top_k(jax.nn.softmax(e_in, axis=-1), TOP_K)
    w = g_val * e_val / jnp.sum(e_val, axis=-1, keepdims=True)
    comb = jnp.einsum('tk,tke->te', w, jax.nn.one_hot(e_idx, EXPERTS_PER_GROUP, dtype=jnp.float32))
    comb = jax.nn.one_hot(g_idx[:, 0], N_GROUPS, dtype=jnp.float32)[:, :, None] * comb[:, None, :]
    y = jnp.zeros((T, D), jnp.float32)
    for grp in range(N_GROUPS):
        h = (jax.nn.silu(jnp.einsum('td,edf->tef', xt, w_gate[grp]))
             * jnp.einsum('td,edf->tef', xt, w_up[grp]))
        h = h * comb[:, grp, :, None].astype(h.dtype)
        y = y + jnp.einsum('tef,efd->td', h, w_down[grp], preferred_element_type=jnp.float32)
    return y.astype(x.dtype).reshape(B, L, D)


def encoder(x, attn_norm, w_in, q_norm, w_uq, kv_norm, w_ukv, lb_param, hg_norm, w_branch, w_out,
            ffn_norm, w_group, b_group, w_expert, b_expert, w_gate, w_up, w_down, final_norm):
    lbs = jnp.cumsum(jax.nn.softmax(lb_param.astype(jnp.float32), axis=0), axis=0)
    for l in range(DEPTH):
        h = rmsnorm(x, attn_norm[l])
        z = h @ w_in[l]
        c_q, c_kv, k_pe, hq, fzf, fzb, hi, hg, ga, gb = jnp.split(z, IN_SPLITS, axis=-1)
        o_a = mla_branch(c_q, c_kv, k_pe, q_norm[l], w_uq[l], kv_norm[l], w_ukv[l])
        o_b = hgrn2_branch(hq, fzf, fzb, hi, hg, lbs[l, 0], lbs[l, 1], hg_norm[l])
        merged = (jax.nn.sigmoid(ga) * (o_a @ w_branch[l, 0])
                  + jax.nn.sigmoid(gb) * (o_b @ w_branch[l, 1]))
        x = x + merged @ w_out[l]
        h = rmsnorm(x, ffn_norm[l])
        x = x + hier_moe(h, w_group[l], b_group[l], w_expert[l], b_expert[l],
                         w_gate[l], w_up[l], w_down[l])
    return rmsnorm(x, final_norm)


def setup_inputs(seed: int = 0) -> dict:
    key = jax.random.key(seed)
    ks = jax.random.split(key, 24)
    f32 = jnp.float32
    nrm = lambda k, shape, fan: jax.random.normal(k, shape, f32) * (fan ** -0.5)
    gain = lambda k, shape: 1.0 + 0.02 * jax.random.normal(k, shape, f32)
    return {
        'x_prompt': jax.random.normal(ks[0], (BATCH, SEQ, D_MODEL), f32),
        'x_sample': jax.random.normal(ks[1], (DEC_BATCH, DEC_SEQ, D_MODEL), f32),
        'attn_norm': gain(ks[2], (DEPTH, D_MODEL)),
        'w_in': nrm(ks[3], (DEPTH, D_MODEL, IN_COLS), D_MODEL),
        'q_norm': gain(ks[4], (DEPTH, Q_LORA)),
        'w_uq': nrm(ks[5], (DEPTH, Q_LORA, MLA_HEADS, QK_DIM), Q_LORA),
        'kv_norm': gain(ks[6], (DEPTH, KV_LORA)),
        'w_ukv': nrm(ks[7], (DEPTH, KV_LORA, MLA_HEADS, QK_NOPE + V_DIM), KV_LORA),
        'lb_param': 0.1 * jax.random.normal(ks[8], (DEPTH + 1, 2, HG_KW), f32),
        'hg_norm': gain(ks[9], (DEPTH, HG_WIDTH)),
        'w_branch': nrm(ks[10], (DEPTH, N_BRANCH, MLA_WIDTH, D_MODEL), MLA_WIDTH),
        'w_out': nrm(ks[11], (DEPTH, D_MODEL, D_MODEL), D_MODEL),
        'ffn_norm': gain(ks[12], (DEPTH, D_MODEL)),
        'w_group': nrm(ks[13], (DEPTH, D_MODEL, N_GROUPS), D_MODEL),
        'b_group': 0.01 * jax.random.normal(ks[14], (DEPTH, N_GROUPS), f32),
        'w_expert': nrm(ks[15], (DEPTH, D_MODEL, N_EXPERTS), D_MODEL),
        'b_expert': 0.01 * jax.random.normal(ks[16], (DEPTH, N_EXPERTS), f32),
        'w_gate': nrm(ks[17], (DEPTH, N_GROUPS, EXPERTS_PER_GROUP, D_MODEL, D_EXPERT), D_MODEL),
        'w_up': nrm(ks[18], (DEPTH, N_GROUPS, EXPERTS_PER_GROUP, D_MODEL, D_EXPERT), D_MODEL),
        'w_down': nrm(ks[19], (DEPTH, N_GROUPS, EXPERTS_PER_GROUP, D_EXPERT, D_MODEL), D_EXPERT),
        'final_norm': gain(ks[20], (D_MODEL,)),
    }


def reference(x_prompt, x_sample, attn_norm, w_in, q_norm, w_uq, kv_norm, w_ukv, lb_param, hg_norm,
              w_branch, w_out, ffn_norm, w_group, b_group, w_expert, b_expert, w_gate, w_up, w_down,
              final_norm):
    y_prompt = encoder(x_prompt, attn_norm, w_in, q_norm, w_uq, kv_norm, w_ukv, lb_param, hg_norm,
                       w_branch, w_out, ffn_norm, w_group, b_group, w_expert, b_expert,
                       w_gate, w_up, w_down, final_norm)
    y_sample = encoder(x_sample, attn_norm, w_in, q_norm, w_uq, kv_norm, w_ukv, lb_param, hg_norm,
                       w_branch, w_out, ffn_norm, w_group, b_group, w_expert, b_expert,
                       w_gate, w_up, w_down, final_norm)
    return (y_prompt, y_sample)
```

```python
import functools
import math

import jax
import jax.numpy as jnp
from jax import lax
from jax.experimental import pallas as pl
from jax.experimental.pallas import tpu as pltpu

D_MODEL = 1024
MLA_HEADS = 8
Q_LORA = 384
KV_LORA = 256
QK_NOPE = 64
QK_ROPE = 32
V_DIM = 64
QK_DIM = QK_NOPE + QK_ROPE
ROPE_THETA = 10000.0
HG_HEADS = 4
HG_DK = 128
HG_DV = 128
HG_KW = HG_HEADS * HG_DK
HG_WIDTH = HG_HEADS * HG_DV
N_GROUPS = 4
EXPERTS_PER_GROUP = 8
N_EXPERTS = N_GROUPS * EXPERTS_PER_GROUP
D_EXPERT = 256
GROUP_FF = EXPERTS_PER_GROUP * D_EXPERT
EPS = 1e-6

LANES = 128
HEAD_PAD = LANES
ATT_W = MLA_HEADS * HEAD_PAD
VMEM_LIMIT = 56 * 1024 * 1024

F32 = jnp.float32
BF16 = jnp.bfloat16


def _dot(a, b):
    return jnp.dot(a, b, preferred_element_type=F32)


def _dot_nt(a, b):
    return lax.dot_general(a, b, (((1,), (1,)), ((), ())), preferred_element_type=F32)


def _dot_tn(a, b):
    return lax.dot_general(a, b, (((0,), (0,)), ((), ())), preferred_element_type=F32)


def _rms(x, g):
    return x * lax.rsqrt(jnp.mean(x * x, axis=-1, keepdims=True) + EPS) * g


def _const_spec(shape):
    zeros = (0,) * len(shape)
    return pl.BlockSpec(shape, lambda *_: zeros, pipeline_mode=pl.Buffered(1))


def _params(sem):
    return pltpu.CompilerParams(dimension_semantics=sem, vmem_limit_bytes=VMEM_LIMIT)


def _in_proj_kernel(x_ref, cos_ref, sin_ref, an_ref, qn_ref, kvn_ref, w_lat_ref, w_kpe_ref, w_hg_ref,
                    w_gate_ref, w_uq_ref, w_ukv_ref,
                    q_ref, k_ref, v_ref, hq_ref, hi_ref, fzf_ref, fzb_ref, hg_ref, ga_ref, gb_ref):
    h = _rms(x_ref[...], an_ref[...]).astype(BF16)
    cos = cos_ref[...]
    sin = sin_ref[...]

    lat = _dot(h, w_lat_ref[...])
    cqn = _rms(lat[:, :Q_LORA], qn_ref[...]).astype(BF16)
    ckvn = _rms(lat[:, Q_LORA:], kvn_ref[...]).astype(BF16)

    kp = _dot(h, w_kpe_ref[...])
    k_rot = kp[:, :HEAD_PAD] * cos + kp[:, HEAD_PAD:] * sin
    q2 = _dot(cqn, w_uq_ref[...])
    kv2 = _dot(ckvn, w_ukv_ref[...])
    scale = QK_DIM ** -0.5
    for hd in range(MLA_HEADS):
        lo, hi = hd * HEAD_PAD, (hd + 1) * HEAD_PAD
        q_ref[:, lo:hi] = ((q2[:, lo:hi] * cos + q2[:, ATT_W + lo:ATT_W + hi] * sin) * scale).astype(BF16)
        k_ref[:, lo:hi] = (kv2[:, lo:hi] + k_rot).astype(BF16)
    v_ref[...] = kv2[:, ATT_W:].astype(BF16)

    hz = _dot(h, w_hg_ref[...])
    hq_ref[...] = (hz[:, :HG_KW] * (HG_DK ** -0.5)).astype(BF16)
    fzf_ref[...] = hz[:, HG_KW:2 * HG_KW]
    fzb_ref[...] = hz[:, 2 * HG_KW:3 * HG_KW]
    hi_ref[...] = hz[:, 3 * HG_KW:3 * HG_KW + HG_WIDTH].astype(BF16)
    hg_ref[...] = hz[:, 3 * HG_KW + HG_WIDTH:].astype(BF16)

    gz = _dot(h, w_gate_ref[...])
    ga_ref[...] = gz[:, :D_MODEL].astype(BF16)
    gb_ref[...] = gz[:, D_MODEL:].astype(BF16)


def _in_proj(x2d, seq_len, tabs, wts, tm):
    t = x2d.shape[0]
    nl = seq_len // tm
    row = lambda w: pl.BlockSpec((tm, w), lambda i: (i, 0))
    tab = pl.BlockSpec((tm, HEAD_PAD), lambda i: (i % nl, 0))
    consts = [wts['an'], wts['qn'], wts['kvn'], wts['w_lat'], wts['w_kpe'], wts['w_hg'], wts['w_gate'],
              wts['w_uq'], wts['w_ukv']]
    out_w = [(ATT_W, BF16), (ATT_W, BF16), (ATT_W, BF16), (HG_KW, BF16), (HG_WIDTH, BF16), (HG_KW, F32),
             (HG_KW, F32), (HG_WIDTH, BF16), (D_MODEL, BF16), (D_MODEL, BF16)]
    return pl.pallas_call(
        _in_proj_kernel,
        name="in_proj",
        grid=(t // tm,),
        in_specs=[row(D_MODEL), tab, tab] + [_const_spec(c.shape) for c in consts],
        out_specs=[row(w) for w, _ in out_w],
        out_shape=[jax.ShapeDtypeStruct((t, w), d) for w, d in out_w],
        compiler_params=_params(("parallel",)),
    )(x2d, tabs[0], tabs[1], *consts)


def _attention_kernel(q_ref, k_ref, v_ref, o_ref, *, tk):
    q = q_ref[0]
    tq = q.shape[0]
    nk = k_ref.shape[1] // tk

    def body(j, carry):
        m, l, acc = carry
        off = pl.multiple_of(j * tk, tk)
        s = _dot_nt(q, k_ref[0, pl.ds(off, tk), :])
        m_new = jnp.maximum(m, jnp.max(s, axis=-1, keepdims=True))
        alpha = jnp.exp(m - m_new)
        p = jnp.exp(s - m_new)
        l = alpha * l + jnp.sum(p, axis=-1, keepdims=True)
        acc = alpha * acc + _dot(p.astype(BF16), v_ref[0, pl.ds(off, tk), :])
        return m_new, l, acc

    init = (jnp.full((tq, 1), -jnp.inf, F32), jnp.zeros((tq, 1), F32), jnp.zeros((tq, HEAD_PAD), F32))
    _, l, acc = lax.fori_loop(0, nk, body, init)
    o_ref[0] = (acc / l).astype(o_ref.dtype)


def _attention(q, k, v, tq, tk):
    b, l, _ = q.shape
    qspec = pl.BlockSpec((1, tq, HEAD_PAD), lambda bi, h, i: (bi, i, h))
    kspec = pl.BlockSpec((1, l, HEAD_PAD), lambda bi, h, i: (bi, 0, h))
    return pl.pallas_call(
        functools.partial(_attention_kernel, tk=tk),
        name="attention",
        grid=(b, MLA_HEADS, l // tq),
        in_specs=[qspec, kspec, kspec],
        out_specs=qspec,
        out_shape=jax.ShapeDtypeStruct((b, l, ATT_W), BF16),
        compiler_params=_params(("parallel", "parallel", "arbitrary")),
    )(q, k, v)


def _gla_chunk(q, v, z, lb, st_ref, tau, reverse):
    c = q.shape[0]
    levels = int(math.log2(c))
    q = q.astype(F32)
    f = lb + (1.0 - lb) * jax.nn.sigmoid(z)
    kk = 1.0 - f
    g = jnp.log(f)

    row = lax.broadcasted_iota(jnp.int32, (c, c), 0)
    col = lax.broadcasted_iota(jnp.int32, (c, c), 1)
    diff = row ^ col
    later = (row < col) if reverse else (row > col)

    def from_prev(a, m):
        return pltpu.roll(a, (c - m) if reverse else m, 0)

    def from_next(a, m):
        return pltpu.roll(a, m if reverse else (c - m), 0)

    a_mat = jnp.where(row == col, _dot_nt(q.astype(BF16), kk.astype(BF16)), 0.0)
    seg = g
    tot = g
    for j in range(levels):
        m = 1 << j
        upper = ((tau >> j) & 1) == 1
        qj = (q * jnp.exp(seg)).astype(BF16)
        kj = (kk * jnp.exp(tot - seg)).astype(BF16)
        pick = later & ((diff >> j) == 1)
        a_mat = a_mat + jnp.where(pick, _dot_nt(qj, kj), 0.0)
        sib = jnp.where(upper, from_prev(tot, m), from_next(tot, m))
        seg = seg + jnp.where(upper, sib, 0.0)
        tot = tot + sib

    st = st_ref[...]
    o = _dot(a_mat.astype(BF16), v) + _dot_nt((q * jnp.exp(seg)).astype(BF16), st.astype(BF16))
    k_end = (kk * jnp.exp(tot - seg)).astype(BF16)
    st_ref[...] = st * jnp.exp(tot[0:1, :]) + _dot_tn(v, k_end)
    return o


def _gla_kernel(qf_ref, vf_ref, zf_ref, qb_ref, vb_ref, zb_ref, lb_ref, of_ref, ob_ref, sf_ref, sb_ref, *, chunk):
    @pl.when(pl.program_id(2) == 0)
    def _():
        sf_ref[...] = jnp.zeros_like(sf_ref)
        sb_ref[...] = jnp.zeros_like(sb_ref)

    tb = qf_ref.shape[1]
    n = tb // chunk
    lbp = lb_ref[...]
    lbe = jnp.exp(lbp - jnp.max(lbp, axis=0, keepdims=True))
    lb = lbe[0] / jnp.sum(lbe, axis=0)
    lbf = lb[0:1, :]
    lbb = lb[1:2, :]
    t_idx = lax.broadcasted_iota(jnp.int32, (chunk, HG_DK), 0)
    for c in range(n):
        sl = pl.ds(c * chunk, chunk)
        of_ref[0, sl, :] = _gla_chunk(qf_ref[0, sl, :], vf_ref[0, sl, :], zf_ref[0, sl, :], lbf, sf_ref,
                                      t_idx, False)
        sl = pl.ds((n - 1 - c) * chunk, chunk)
        ob_ref[0, sl, :] = _gla_chunk(qb_ref[0, sl, :], vb_ref[0, sl, :], zb_ref[0, sl, :], lbb, sb_ref,
                                      chunk - 1 - t_idx, True)


def _gla(hq, hi, fzf, fzb, lbs, tb, chunk):
    b, l, _ = hq.shape
    nt = l // tb
    fwd = pl.BlockSpec((1, tb, HG_DK), lambda bi, h, i: (bi, i, h))
    bwd = pl.BlockSpec((1, tb, HG_DK), lambda bi, h, i: (bi, nt - 1 - i, h))
    lbspec = pl.BlockSpec((lbs.shape[0], 2, HG_DK), lambda bi, h, i: (0, 0, h))
    out = jax.ShapeDtypeStruct((b, l, HG_WIDTH), F32)
    return pl.pallas_call(
        functools.partial(_gla_kernel, chunk=chunk),
        name="gla",
        grid=(b, HG_HEADS, nt),
        in_specs=[fwd, fwd, fwd, bwd, bwd, bwd, lbspec],
        out_specs=[fwd, bwd],
        out_shape=[out, out],
        scratch_shapes=[pltpu.VMEM((HG_DV, HG_DK), F32), pltpu.VMEM((HG_DV, HG_DK), F32)],
        compiler_params=_params(("parallel", "parallel", "arbitrary")),
    )(hq, hi, fzf, hq, hi, fzb, lbs)


def _route(logits):
    lane = lax.broadcasted_iota(jnp.int32, logits.shape, 1)
    neg = -jnp.inf
    big = jnp.int32(LANES)

    is_g = (lane >= N_EXPERTS) & (lane < N_EXPERTS + N_GROUPS)
    gl = jnp.where(is_g, logits, neg)
    ge = jnp.exp(gl - jnp.max(gl, axis=-1, keepdims=True))
    gp = ge / jnp.sum(ge, axis=-1, keepdims=True)
    g_val = jnp.max(gp, axis=-1, keepdims=True)
    g_idx = jnp.min(jnp.where(is_g & (gp == g_val), lane, big), axis=-1, keepdims=True) - N_EXPERTS

    sel = (lane < N_EXPERTS) & ((lane >> 3) == g_idx)
    el = jnp.where(sel, logits, neg)
    ee = jnp.exp(el - jnp.max(el, axis=-1, keepdims=True))
    ep = ee / jnp.sum(ee, axis=-1, keepdims=True)
    v1 = jnp.max(ep, axis=-1, keepdims=True)
    i1 = jnp.min(jnp.where(sel & (ep == v1), lane, big), axis=-1, keepdims=True)
    rest = jnp.where(sel & (lane != i1), ep, -1.0)
    v2 = jnp.max(rest, axis=-1, keepdims=True)
    i2 = jnp.min(jnp.where(rest == v2, lane, big), axis=-1, keepdims=True)
    den = v1 + v2
    return jnp.where(lane == i1, g_val * v1 / den, 0.0) + jnp.where(lane == i2, g_val * v2 / den, 0.0)


def _merge_kernel(x_ref, oa_ref, of_ref, ob_ref, hg_ref, ga_ref, gb_ref, hgn_ref, ffn_ref, wb0_ref, wb1_ref,
                  wout_ref, wr_ref, br_ref, x1_ref, h2_ref, comb_ref):
    o = of_ref[...] + ob_ref[...]
    hgn = hgn_ref[...]
    parts = []
    for hd in range(HG_HEADS):
        lo, hi = hd * HG_DV, (hd + 1) * HG_DV
        parts.append(_rms(o[:, lo:hi], hgn[:, lo:hi]))
    o_b = (jnp.concatenate(parts, axis=-1) * jax.nn.silu(hg_ref[...].astype(F32))).astype(BF16)
    merged = (jax.nn.sigmoid(ga_ref[...].astype(F32)) * _dot(oa_ref[...], wb0_ref[...])
              + jax.nn.sigmoid(gb_ref[...].astype(F32)) * _dot(o_b, wb1_ref[...]))
    x1 = x_ref[...] + _dot(merged.astype(BF16), wout_ref[...])
    x1_ref[...] = x1
    h2 = _rms(x1, ffn_ref[...])
    h2_ref[...] = h2.astype(BF16)
    logits = jnp.dot(h2, wr_ref[...], preferred_element_type=F32, precision=lax.Precision.HIGHEST) + br_ref[...]
    comb_ref[...] = _route(logits)


def _merge(x2d, oa, of, ob, hg, ga, gb, wts, tm):
    t = x2d.shape[0]
    row = lambda w: pl.BlockSpec((tm, w), lambda i: (i, 0))
    consts = [wts['hgn'], wts['ffn'], wts['wb0'], wts['wb1'], wts['w_out'], wts['w_router'], wts['b_router']]
    return pl.pallas_call(
        _merge_kernel,
        name="merge",
        grid=(t // tm,),
        in_specs=[row(D_MODEL), row(ATT_W), row(HG_WIDTH), row(HG_WIDTH), row(HG_WIDTH), row(D_MODEL),
                  row(D_MODEL)] + [_const_spec(c.shape) for c in consts],
        out_specs=[row(D_MODEL), row(D_MODEL), row(LANES)],
        out_shape=[jax.ShapeDtypeStruct((t, D_MODEL), F32), jax.ShapeDtypeStruct((t, D_MODEL), BF16),
                   jax.ShapeDtypeStruct((t, LANES), F32)],
        compiler_params=_params(("parallel",)),
    )(x2d, oa, of, ob, hg, ga, gb, *consts)


def _moe_kernel(x1_ref, h2_ref, comb_ref, wg_ref, wu_ref, wd_ref, fn_ref, y_ref, acc_ref, hh_ref):
    grp = pl.program_id(1)

    @pl.when(grp == 0)
    def _():
        acc_ref[...] = jnp.zeros_like(acc_ref)

    h2 = h2_ref[...]
    gate = _dot(h2, wg_ref[0])
    up = _dot(h2, wu_ref[0])
    comb = comb_ref[...]
    lane = lax.broadcasted_iota(jnp.int32, comb.shape, 1)
    for e in range(EXPERTS_PER_GROUP):
        lo, hi = e * D_EXPERT, (e + 1) * D_EXPERT
        w_e = jnp.sum(jnp.where(lane == grp * EXPERTS_PER_GROUP + e, comb, 0.0), axis=-1, keepdims=True)
        hh_ref[:, lo:hi] = (jax.nn.silu(gate[:, lo:hi]) * up[:, lo:hi] * w_e).astype(BF16)
    acc_ref[...] += _dot(hh_ref[...], wd_ref[0])

    @pl.when(grp == N_GROUPS - 1)
    def _():
        y_ref[...] = _rms(x1_ref[...] + acc_ref[...], fn_ref[...])


def _moe(x1, h2, comb, wts, tm):
    t = x1.shape[0]
    row = lambda w: pl.BlockSpec((tm, w), lambda i, g: (i, 0))
    return pl.pallas_call(
        _moe_kernel,
        name="moe",
        grid=(t // tm, N_GROUPS),
        in_specs=[row(D_MODEL), row(D_MODEL), row(LANES),
                  pl.BlockSpec((1, D_MODEL, GROUP_FF), lambda i, g: (g, 0, 0)),
                  pl.BlockSpec((1, D_MODEL, GROUP_FF), lambda i, g: (g, 0, 0)),
                  pl.BlockSpec((1, GROUP_FF, D_MODEL), lambda i, g: (g, 0, 0)),
                  pl.BlockSpec((1, D_MODEL), lambda i, g: (0, 0))],
        out_specs=row(D_MODEL),
        out_shape=jax.ShapeDtypeStruct((t, D_MODEL), F32),
        scratch_shapes=[pltpu.VMEM((tm, D_MODEL), F32), pltpu.VMEM((tm, GROUP_FF), BF16)],
        compiler_params=_params(("parallel", "arbitrary")),
    )(x1, h2, comb, wts['w_gate_e'], wts['w_up_e'], wts['w_down_e'], wts['fn'])


def _rope_tables(seq_len):
    pos = jnp.arange(seq_len, dtype=F32)
    inv_freq = 1.0 / (ROPE_THETA ** (jnp.arange(0, QK_ROPE, 2, dtype=F32) / QK_ROPE))
    ang = pos[:, None] * inv_freq[None, :]
    cos, sin = jnp.cos(ang), jnp.sin(ang)
    pad = jnp.zeros((seq_len, HEAD_PAD - QK_DIM), F32)
    cos_t = jnp.concatenate([jnp.ones((seq_len, QK_NOPE), F32), cos, cos, pad], axis=-1)
    sin_t = jnp.concatenate([jnp.zeros((seq_len, QK_NOPE), F32), sin, sin, pad], axis=-1)
    return cos_t, sin_t


def _rot_half_cols(w):
    half = QK_ROPE // 2
    return jnp.concatenate([-w[..., half:], w[..., :half]], axis=-1)


def _prep_weights(attn_norm, w_in, q_norm, w_uq, kv_norm, w_ukv, lb_param, hg_norm, w_branch, w_out, ffn_norm,
                  w_group, b_group, w_expert, b_expert, w_gate, w_up, w_down, final_norm):
    w_in = w_in[0]
    o_kpe = Q_LORA + KV_LORA
    o_hg = o_kpe + QK_ROPE
    o_gate = o_hg + 3 * HG_KW + 2 * HG_WIDTH
    w_kpe = w_in[:, o_kpe:o_hg]
    zpad = lambda n: jnp.zeros((D_MODEL, n), F32)
    place = lambda w: jnp.concatenate([zpad(QK_NOPE), w, zpad(HEAD_PAD - QK_DIM)], axis=-1)
    w_kpe2 = jnp.concatenate([place(w_kpe), place(_rot_half_cols(w_kpe))], axis=-1)

    uq = w_uq[0]
    zq = lambda n: jnp.zeros((Q_LORA, MLA_HEADS, n), F32)
    uq_full = jnp.concatenate([uq, zq(HEAD_PAD - QK_DIM)], axis=-1)
    uq_rot = jnp.concatenate([zq(QK_NOPE), _rot_half_cols(uq[..., QK_NOPE:]), zq(HEAD_PAD - QK_DIM)], axis=-1)
    w_uq2 = jnp.concatenate([uq_full.reshape(Q_LORA, ATT_W), uq_rot.reshape(Q_LORA, ATT_W)], axis=-1)

    ukv = w_ukv[0]
    zk = lambda n: jnp.zeros((KV_LORA, MLA_HEADS, n), F32)
    uk = jnp.concatenate([ukv[..., :QK_NOPE], zk(HEAD_PAD - QK_NOPE)], axis=-1)
    uv = jnp.concatenate([ukv[..., QK_NOPE:], zk(HEAD_PAD - V_DIM)], axis=-1)
    w_ukv2 = jnp.concatenate([uk.reshape(KV_LORA, ATT_W), uv.reshape(KV_LORA, ATT_W)], axis=-1)

    wb0 = w_branch[0, 0].reshape(MLA_HEADS, V_DIM, D_MODEL)
    wb0 = jnp.concatenate([wb0, jnp.zeros((MLA_HEADS, HEAD_PAD - V_DIM, D_MODEL), F32)], axis=1)

    w_router = jnp.concatenate([w_expert[0], w_group[0],
                                jnp.zeros((D_MODEL, LANES - N_EXPERTS - N_GROUPS), F32)], axis=-1)
    b_router = jnp.concatenate([b_expert[0], b_group[0], jnp.zeros((LANES - N_EXPERTS - N_GROUPS,), F32)])

    ff = lambda w: jnp.transpose(w[0], (0, 2, 1, 3)).reshape(N_GROUPS, D_MODEL, GROUP_FF).astype(BF16)
    lbs = lb_param.astype(F32)
    return {
        'an': attn_norm[0][None], 'qn': q_norm[0][None], 'kvn': kv_norm[0][None],
        'w_lat': w_in[:, :o_kpe].astype(BF16), 'w_kpe': w_kpe2.astype(BF16),
        'w_hg': w_in[:, o_hg:o_gate].astype(BF16), 'w_gate': w_in[:, o_gate:].astype(BF16),
        'w_uq': w_uq2.astype(BF16), 'w_ukv': w_ukv2.astype(BF16), 'lbs': lbs,
        'hgn': hg_norm[0][None], 'ffn': ffn_norm[0][None],
        'wb0': wb0.reshape(ATT_W, D_MODEL).astype(BF16), 'wb1': w_branch[0, 1].astype(BF16),
        'w_out': w_out[0].astype(BF16), 'w_router': w_router, 'b_router': b_router[None],
        'w_gate_e': ff(w_gate), 'w_up_e': ff(w_up),
        'w_down_e': w_down[0].reshape(N_GROUPS, GROUP_FF, D_MODEL).astype(BF16), 'fn': final_norm[None],
    }


TOKEN_TILE = 512
ATT_TQ = 512
ATT_TK = 512
GLA_ROWS = 256
GLA_CHUNK = 64


def _tile(n, pref):
    t = min(n, pref)
    assert n % t == 0
    return t


def _encoder(x, wts):
    b, l, d = x.shape
    t = b * l
    x2d = x.reshape(t, d)
    tm = _tile(l, TOKEN_TILE)
    q, k, v, hq, hi, fzf, fzb, hg, ga, gb = _in_proj(x2d, l, _rope_tables(l), wts, tm)
    r3 = lambda a: a.reshape(b, l, a.shape[-1])
    oa = _attention(r3(q), r3(k), r3(v), _tile(l, ATT_TQ), _tile(l, ATT_TK))
    of, ob = _gla(r3(hq), r3(hi), r3(fzf), r3(fzb), wts['lbs'], _tile(l, GLA_ROWS), GLA_CHUNK)
    x1, h2, comb = _merge(x2d, oa.reshape(t, ATT_W), of.reshape(t, HG_WIDTH), ob.reshape(t, HG_WIDTH),
                          hg, ga, gb, wts, tm)
    y = _moe(x1, h2, comb, wts, tm)
    return y.reshape(b, l, d)


def kernel(x_prompt, x_sample, attn_norm, w_in, q_norm, w_uq, kv_norm, w_ukv, lb_param, hg_norm, w_branch, w_out,
           ffn_norm, w_group, b_group, w_expert, b_expert, w_gate, w_up, w_down, final_norm):
    wts = _prep_weights(attn_norm, w_in, q_norm, w_uq, kv_norm, w_ukv, lb_param, hg_norm, w_branch, w_out,
                        ffn_norm, w_group, b_group, w_expert, b_expert, w_gate, w_up, w_down, final_norm)
    return (_encoder(x_prompt, wts), _encoder(x_sample, wts))
```

```python
import functools
import math

import jax
import jax.numpy as jnp
from jax import lax
from jax.experimental import pallas as pl
from jax.experimental.pallas import tpu as pltpu

D_MODEL = 1024
MLA_HEADS = 8
Q_LORA = 384
KV_LORA = 256
QK_NOPE = 64
QK_ROPE = 32
V_DIM = 64
QK_DIM = QK_NOPE + QK_ROPE
ROPE_THETA = 10000.0
HG_HEADS = 4
HG_DK = 128
HG_DV = 128
HG_KW = HG_HEADS * HG_DK
HG_WIDTH = HG_HEADS * HG_DV
N_GROUPS = 4
EXPERTS_PER_GROUP = 8
N_EXPERTS = N_GROUPS * EXPERTS_PER_GROUP
D_EXPERT = 256
GROUP_FF = EXPERTS_PER_GROUP * D_EXPERT
EPS = 1e-6

LANES = 128
HEAD_PAD = LANES
ATT_W = MLA_HEADS * HEAD_PAD
VMEM_LIMIT = 56 * 1024 * 1024

F32 = jnp.float32
BF16 = jnp.bfloat16


def _dot(a, b):
    return jnp.dot(a, b, preferred_element_type=F32)


def _dot_nt(a, b):
    return lax.dot_general(a, b, (((1,), (1,)), ((), ())), preferred_element_type=F32)


def _dot_tn(a, b):
    return lax.dot_general(a, b, (((0,), (0,)), ((), ())), preferred_element_type=F32)


def _rms(x, g):
    return x * lax.rsqrt(jnp.mean(x * x, axis=-1, keepdims=True) + EPS) * g


def _const_spec(shape):
    zeros = (0,) * len(shape)
    return pl.BlockSpec(shape, lambda *_: zeros, pipeline_mode=pl.Buffered(1))


def _params(sem):
    return pltpu.CompilerParams(dimension_semantics=sem, vmem_limit_bytes=VMEM_LIMIT)


def _in_proj_kernel(x_ref, cos_ref, sin_ref, an_ref, qn_ref, kvn_ref, w_lat_ref, w_kpe_ref, w_hg_ref,
                    w_gate_ref, w_uq_ref, w_ukv_ref,
                    q_ref, k_ref, v_ref, hq_ref, hi_ref, fzf_ref, fzb_ref, hg_ref, ga_ref, gb_ref):
    h = _rms(x_ref[...], an_ref[...]).astype(BF16)
    cos = cos_ref[...]
    sin = sin_ref[...]

    lat = _dot(h, w_lat_ref[...])
    cqn = _rms(lat[:, :Q_LORA], qn_ref[...]).astype(BF16)
    ckvn = _rms(lat[:, Q_LORA:], kvn_ref[...]).astype(BF16)

    kp = _dot(h, w_kpe_ref[...])
    k_rot = kp[:, :HEAD_PAD] * cos + kp[:, HEAD_PAD:] * sin
    q2 = _dot(cqn, w_uq_ref[...])
    kv2 = _dot(ckvn, w_ukv_ref[...])
    scale = QK_DIM ** -0.5 * math.log2(math.e)
    for hd in range(MLA_HEADS):
        lo, hi = hd * HEAD_PAD, (hd + 1) * HEAD_PAD
        q_ref[:, lo:hi] = ((q2[:, lo:hi] * cos + q2[:, ATT_W + lo:ATT_W + hi] * sin) * scale).astype(BF16)
        k_ref[:, lo:hi] = (kv2[:, lo:hi] + k_rot).astype(BF16)
    lane = lax.broadcasted_iota(jnp.int32, (1, ATT_W), 1)
    v_ref[...] = jnp.where(lane % HEAD_PAD == V_DIM, 1.0, kv2[:, ATT_W:]).astype(BF16)

    hz = _dot(h, w_hg_ref[...])
    hq_ref[...] = (hz[:, :HG_KW] * (HG_DK ** -0.5)).astype(BF16)
    fzf_ref[...] = hz[:, HG_KW:2 * HG_KW]
    fzb_ref[...] = hz[:, 2 * HG_KW:3 * HG_KW]
    hi_ref[...] = hz[:, 3 * HG_KW:3 * HG_KW + HG_WIDTH].astype(BF16)
    hg_ref[...] = hz[:, 3 * HG_KW + HG_WIDTH:].astype(BF16)

    gz = _dot(h, w_gate_ref[...])
    ga_ref[...] = gz[:, :D_MODEL].astype(BF16)
    gb_ref[...] = gz[:, D_MODEL:].astype(BF16)


def _in_proj(x2d, seq_len, tabs, wts, tm):
    t = x2d.shape[0]
    nl = seq_len // tm
    row = lambda w: pl.BlockSpec((tm, w), lambda i: (i, 0))
    tab = pl.BlockSpec((tm, HEAD_PAD), lambda i: (i % nl, 0))
    consts = [wts['an'], wts['qn'], wts['kvn'], wts['w_lat'], wts['w_kpe'], wts['w_hg'], wts['w_gate'],
              wts['w_uq'], wts['w_ukv']]
    out_w = [(ATT_W, BF16), (ATT_W, BF16), (ATT_W, BF16), (HG_KW, BF16), (HG_WIDTH, BF16), (HG_KW, F32),
             (HG_KW, F32), (HG_WIDTH, BF16), (D_MODEL, BF16), (D_MODEL, BF16)]
    return pl.pallas_call(
        _in_proj_kernel,
        name="in_proj",
        grid=(t // tm,),
        in_specs=[row(D_MODEL), tab, tab] + [_const_spec(c.shape) for c in consts],
        out_specs=[row(w) for w, _ in out_w],
        out_shape=[jax.ShapeDtypeStruct((t, w), d) for w, d in out_w],
        compiler_params=_params(("parallel",)),
    )(x2d, tabs[0], tabs[1], *consts)


def _attention_kernel(q_ref, k_ref, v_ref, o_ref, sa_ref, sb_ref, m_ref, acc_ref, *, tk):
    q = q_ref[0]
    nk = k_ref.shape[1] // tk

    def scores(j):
        off = pl.multiple_of(j * tk, tk)
        return _dot_nt(q, k_ref[0, pl.ds(off, tk), :])

    def absorb(s_ref, j):
        off = pl.multiple_of(j * tk, tk)
        s = s_ref[...]
        m_old = m_ref[...]
        m_new = jnp.maximum(m_old, jnp.max(s, axis=-1, keepdims=True))
        p = jnp.exp2(s - m_new).astype(BF16)
        acc_ref[...] = jnp.exp2(m_old - m_new) * acc_ref[...] + _dot(p, v_ref[0, pl.ds(off, tk), :])
        m_ref[...] = m_new

    m_ref[...] = jnp.full(m_ref.shape, -jnp.inf, F32)
    acc_ref[...] = jnp.zeros(acc_ref.shape, F32)
    sa_ref[...] = scores(0)

    def body(i, carry):
        j = 2 * i
        sb_ref[...] = scores(j + 1)
        absorb(sa_ref, j)
        sa_ref[...] = scores(j + 2)
        absorb(sb_ref, j + 1)
        return carry

    lax.fori_loop(0, nk // 2 - 1, body, 0)
    sb_ref[...] = scores(nk - 1)
    absorb(sa_ref, nk - 2)
    absorb(sb_ref, nk - 1)
    acc = acc_ref[...]
    o_ref[0] = (acc / acc[:, V_DIM:V_DIM + 1]).astype(o_ref.dtype)


def _attention(q, k, v, tq, tk):
    b, l, _ = q.shape
    qspec = pl.BlockSpec((1, tq, HEAD_PAD), lambda bi, h, i: (bi, i, h))
    kspec = pl.BlockSpec((1, l, HEAD_PAD), lambda bi, h, i: (bi, 0, h))
    return pl.pallas_call(
        functools.partial(_attention_kernel, tk=tk),
        name="attention",
        grid=(b, MLA_HEADS, l // tq),
        in_specs=[qspec, kspec, kspec],
        out_specs=qspec,
        out_shape=jax.ShapeDtypeStruct((b, l, ATT_W), BF16),
        scratch_shapes=[pltpu.VMEM((tq, tk), F32), pltpu.VMEM((tq, tk), F32), pltpu.VMEM((tq, 1), F32),
                        pltpu.VMEM((tq, HEAD_PAD), F32)],
        compiler_params=_params(("parallel", "parallel", "arbitrary")),
    )(q, k, v)


def _gla_chunk(q, v, z, lb, st_ref, tau, reverse):
    c = q.shape[0]
    levels = int(math.log2(c))
    q = q.astype(F32)
    f = lb + (1.0 - lb) * jax.nn.sigmoid(z)
    kk = 1.0 - f
    g = jnp.log(f)

    row = lax.broadcasted_iota(jnp.int32, (c, c), 0)
    col = lax.broadcasted_iota(jnp.int32, (c, c), 1)
    diff = row ^ col
    later = (row < col) if reverse else (row > col)

    def from_prev(a, m):
        return pltpu.roll(a, (c - m) if reverse else m, 0)

    def from_next(a, m):
        return pltpu.roll(a, m if reverse else (c - m), 0)

    a_mat = jnp.where(row == col, _dot_nt(q.astype(BF16), kk.astype(BF16)), 0.0)
    seg = g
    tot = g
    for j in range(levels):
        m = 1 << j
        upper = ((tau >> j) & 1) == 1
        qj = (q * jnp.exp(seg)).astype(BF16)
        kj = (kk * jnp.exp(tot - seg)).astype(BF16)
        pick = later & ((diff >> j) == 1)
        a_mat = a_mat + jnp.where(pick, _dot_nt(qj, kj), 0.0)
        sib = jnp.where(upper, from_prev(tot, m), from_next(tot, m))
        seg = seg + jnp.where(upper, sib, 0.0)
        tot = tot + sib

    st = st_ref[...]
    o = _dot(a_mat.astype(BF16), v) + _dot_nt((q * jnp.exp(seg)).astype(BF16), st.astype(BF16))
    k_end = (kk * jnp.exp(tot - seg)).astype(BF16)
    st_ref[...] = st * jnp.exp(tot[0:1, :]) + _dot_tn(v, k_end)
    return o


def _gla_kernel(qf_ref, vf_ref, zf_ref, qb_ref, vb_ref, zb_ref, lb_ref, of_ref, ob_ref, sf_ref, sb_ref, *, chunk):
    @pl.when(pl.program_id(2) == 0)
    def _():
        sf_ref[...] = jnp.zeros_like(sf_ref)
        sb_ref[...] = jnp.zeros_like(sb_ref)

    tb = qf_ref.shape[1]
    n = tb // chunk
    lbp = lb_ref[...]
    lbe = jnp.exp(lbp - jnp.max(lbp, axis=0, keepdims=True))
    lb = lbe[0] / jnp.sum(lbe, axis=0)
    lbf = lb[0:1, :]
    lbb = lb[1:2, :]
    t_idx = lax.broadcasted_iota(jnp.int32, (chunk, HG_DK), 0)
    for c in range(n):
        sl = pl.ds(c * chunk, chunk)
        of_ref[0, sl, :] = _gla_chunk(qf_ref[0, sl, :], vf_ref[0, sl, :], zf_ref[0, sl, :], lbf, sf_ref,
                                      t_idx, False)
        sl = pl.ds((n - 1 - c) * chunk, chunk)
        ob_ref[0, sl, :] = _gla_chunk(qb_ref[0, sl, :], vb_ref[0, sl, :], zb_ref[0, sl, :], lbb, sb_ref,
                                      chunk - 1 - t_idx, True)


def _gla(hq, hi, fzf, fzb, lbs, tb, chunk):
    b, l, _ = hq.shape
    nt = l // tb
    fwd = pl.BlockSpec((1, tb, HG_DK), lambda bi, h, i: (bi, i, h))
    bwd = pl.BlockSpec((1, tb, HG_DK), lambda bi, h, i: (bi, nt - 1 - i, h))
    lbspec = pl.BlockSpec((lbs.shape[0], 2, HG_DK), lambda bi, h, i: (0, 0, h))
    out = jax.ShapeDtypeStruct((b, l, HG_WIDTH), F32)
    return pl.pallas_call(
        functools.partial(_gla_kernel, chunk=chunk),
        name="gla",
        grid=(b, HG_HEADS, nt),
        in_specs=[fwd, fwd, fwd, bwd, bwd, bwd, lbspec],
        out_specs=[fwd, bwd],
        out_shape=[out, out],
        scratch_shapes=[pltpu.VMEM((HG_DV, HG_DK), F32), pltpu.VMEM((HG_DV, HG_DK), F32)],
        compiler_params=_params(("parallel", "parallel", "arbitrary")),
    )(hq, hi, fzf, hq, hi, fzb, lbs)


def _route(logits):
    lane = lax.broadcasted_iota(jnp.int32, logits.shape, 1)
    neg = -jnp.inf
    big = jnp.int32(LANES)

    is_g = (lane >= N_EXPERTS) & (lane < N_EXPERTS + N_GROUPS)
    gl = jnp.where(is_g, logits, neg)
    ge = jnp.exp(gl - jnp.max(gl, axis=-1, keepdims=True))
    gp = ge / jnp.sum(ge, axis=-1, keepdims=True)
    g_val = jnp.max(gp, axis=-1, keepdims=True)
    g_idx = jnp.min(jnp.where(is_g & (gp == g_val), lane, big), axis=-1, keepdims=True) - N_EXPERTS

    sel = (lane < N_EXPERTS) & ((lane >> 3) == g_idx)
    el = jnp.where(sel, logits, neg)
    ee = jnp.exp(el - jnp.max(el, axis=-1, keepdims=True))
    ep = ee / jnp.sum(ee, axis=-1, keepdims=True)
    v1 = jnp.max(ep, axis=-1, keepdims=True)
    i1 = jnp.min(jnp.where(sel & (ep == v1), lane, big), axis=-1, keepdims=True)
    rest = jnp.where(sel & (lane != i1), ep, -1.0)
    v2 = jnp.max(rest, axis=-1, keepdims=True)
    i2 = jnp.min(jnp.where(rest == v2, lane, big), axis=-1, keepdims=True)
    den = v1 + v2
    return jnp.where(lane == i1, g_val * v1 / den, 0.0) + jnp.where(lane == i2, g_val * v2 / den, 0.0)


def _merge_kernel(x_ref, oa_ref, of_ref, ob_ref, hg_ref, ga_ref, gb_ref, hgn_ref, ffn_ref, wb0_ref, wb1_ref,
                  wout_ref, wr_ref, br_ref, x1_ref, h2_ref, comb_ref):
    o = of_ref[...] + ob_ref[...]
    hgn = hgn_ref[...]
    parts = []
    for hd in range(HG_HEADS):
        lo, hi = hd * HG_DV, (hd + 1) * HG_DV
        parts.append(_rms(o[:, lo:hi], hgn[:, lo:hi]))
    o_b = (jnp.concatenate(parts, axis=-1) * jax.nn.silu(hg_ref[...].astype(F32))).astype(BF16)
    merged = (jax.nn.sigmoid(ga_ref[...].astype(F32)) * _dot(oa_ref[...], wb0_ref[...])
              + jax.nn.sigmoid(gb_ref[...].astype(F32)) * _dot(o_b, wb1_ref[...]))
    x1 = x_ref[...] + _dot(merged.astype(BF16), wout_ref[...])
    x1_ref[...] = x1
    h2 = _rms(x1, ffn_ref[...])
    h2_ref[...] = h2.astype(BF16)
    logits = jnp.dot(h2, wr_ref[...], preferred_element_type=F32, precision=lax.Precision.HIGHEST) + br_ref[...]
    comb_ref[...] = _route(logits)


def _merge(x2d, oa, of, ob, hg, ga, gb, wts, tm):
    t = x2d.shape[0]
    row = lambda w: pl.BlockSpec((tm, w), lambda i: (i, 0))
    consts = [wts['hgn'], wts['ffn'], wts['wb0'], wts['wb1'], wts['w_out'], wts['w_router'], wts['b_router']]
    return pl.pallas_call(
        _merge_kernel,
        name="merge",
        grid=(t // tm,),
        in_specs=[row(D_MODEL), row(ATT_W), row(HG_WIDTH), row(HG_WIDTH), row(HG_WIDTH), row(D_MODEL),
                  row(D_MODEL)] + [_const_spec(c.shape) for c in consts],
        out_specs=[row(D_MODEL), row(D_MODEL), row(LANES)],
        out_shape=[jax.ShapeDtypeStruct((t, D_MODEL), F32), jax.ShapeDtypeStruct((t, D_MODEL), BF16),
                   jax.ShapeDtypeStruct((t, LANES), F32)],
        compiler_params=_params(("parallel",)),
    )(x2d, oa, of, ob, hg, ga, gb, *consts)


def _moe_kernel(x1_ref, h2_ref, comb_ref, wg_ref, wu_ref, wd_ref, fn_ref, y_ref, acc_ref, hh_ref):
    grp = pl.program_id(1)

    @pl.when(grp == 0)
    def _():
        acc_ref[...] = jnp.zeros_like(acc_ref)

    h2 = h2_ref[...]
    gate = _dot(h2, wg_ref[0])
    up = _dot(h2, wu_ref[0])
    comb = comb_ref[...]
    lane = lax.broadcasted_iota(jnp.int32, comb.shape, 1)
    for e in range(EXPERTS_PER_GROUP):
        lo, hi = e * D_EXPERT, (e + 1) * D_EXPERT
        w_e = jnp.sum(jnp.where(lane == grp * EXPERTS_PER_GROUP + e, comb, 0.0), axis=-1, keepdims=True)
        hh_ref[:, lo:hi] = (jax.nn.silu(gate[:, lo:hi]) * up[:, lo:hi] * w_e).astype(BF16)
    acc_ref[...] += _dot(hh_ref[...], wd_ref[0])

    @pl.when(grp == N_GROUPS - 1)
    def _():
        y_ref[...] = _rms(x1_ref[...] + acc_ref[...], fn_ref[...])


def _moe(x1, h2, comb, wts, tm):
    t = x1.shape[0]
    row = lambda w: pl.BlockSpec((tm, w), lambda i, g: (i, 0))
    return pl.pallas_call(
        _moe_kernel,
        name="moe",
        grid=(t // tm, N_GROUPS),
        in_specs=[row(D_MODEL), row(D_MODEL), row(LANES),
                  pl.BlockSpec((1, D_MODEL, GROUP_FF), lambda i, g: (g, 0, 0)),
                  pl.BlockSpec((1, D_MODEL, GROUP_FF), lambda i, g: (g, 0, 0)),
                  pl.BlockSpec((1, GROUP_FF, D_MODEL), lambda i, g: (g, 0, 0)),
                  pl.BlockSpec((1, D_MODEL), lambda i, g: (0, 0))],
        out_specs=row(D_MODEL),
        out_shape=jax.ShapeDtypeStruct((t, D_MODEL), F32),
        scratch_shapes=[pltpu.VMEM((tm, D_MODEL), F32), pltpu.VMEM((tm, GROUP_FF), BF16)],
        compiler_params=_params(("parallel", "arbitrary")),
    )(x1, h2, comb, wts['w_gate_e'], wts['w_up_e'], wts['w_down_e'], wts['fn'])


def _rope_tables(seq_len):
    pos = jnp.arange(seq_len, dtype=F32)
    inv_freq = 1.0 / (ROPE_THETA ** (jnp.arange(0, QK_ROPE, 2, dtype=F32) / QK_ROPE))
    ang = pos[:, None] * inv_freq[None, :]
    cos, sin = jnp.cos(ang), jnp.sin(ang)
    pad = jnp.zeros((seq_len, HEAD_PAD - QK_DIM), F32)
    cos_t = jnp.concatenate([jnp.ones((seq_len, QK_NOPE), F32), cos, cos, pad], axis=-1)
    sin_t = jnp.concatenate([jnp.zeros((seq_len, QK_NOPE), F32), sin, sin, pad], axis=-1)
    return cos_t, sin_t


def _rot_half_cols(w):
    half = QK_ROPE // 2
    return jnp.concatenate([-w[..., half:], w[..., :half]], axis=-1)


def _prep_weights(attn_norm, w_in, q_norm, w_uq, kv_norm, w_ukv, lb_param, hg_norm, w_branch, w_out, ffn_norm,
                  w_group, b_group, w_expert, b_expert, w_gate, w_up, w_down, final_norm):
    w_in = w_in[0]
    o_kpe = Q_LORA + KV_LORA
    o_hg = o_kpe + QK_ROPE
    o_gate = o_hg + 3 * HG_KW + 2 * HG_WIDTH
    w_kpe = w_in[:, o_kpe:o_hg]
    zpad = lambda n: jnp.zeros((D_MODEL, n), F32)
    place = lambda w: jnp.concatenate([zpad(QK_NOPE), w, zpad(HEAD_PAD - QK_DIM)], axis=-1)
    w_kpe2 = jnp.concatenate([place(w_kpe), place(_rot_half_cols(w_kpe))], axis=-1)

    uq = w_uq[0]
    zq = lambda n: jnp.zeros((Q_LORA, MLA_HEADS, n), F32)
    uq_full = jnp.concatenate([uq, zq(HEAD_PAD - QK_DIM)], axis=-1)
    uq_rot = jnp.concatenate([zq(QK_NOPE), _rot_half_cols(uq[..., QK_NOPE:]), zq(HEAD_PAD - QK_DIM)], axis=-1)
    w_uq2 = jnp.concatenate([uq_full.reshape(Q_LORA, ATT_W), uq_rot.reshape(Q_LORA, ATT_W)], axis=-1)

    ukv = w_ukv[0]
    zk = lambda n: jnp.zeros((KV_LORA, MLA_HEADS, n), F32)
    uk = jnp.concatenate([ukv[..., :QK_NOPE], zk(HEAD_PAD - QK_NOPE)], axis=-1)
    uv = jnp.concatenate([ukv[..., QK_NOPE:], zk(HEAD_PAD - V_DIM)], axis=-1)
    w_ukv2 = jnp.concatenate([uk.reshape(KV_LORA, ATT_W), uv.reshape(KV_LORA, ATT_W)], axis=-1)

    wb0 = w_branch[0, 0].reshape(MLA_HEADS, V_DIM, D_MODEL)
    wb0 = jnp.concatenate([wb0, jnp.zeros((MLA_HEADS, HEAD_PAD - V_DIM, D_MODEL), F32)], axis=1)

    w_router = jnp.concatenate([w_expert[0], w_group[0],
                                jnp.zeros((D_MODEL, LANES - N_EXPERTS - N_GROUPS), F32)], axis=-1)
    b_router = jnp.concatenate([b_expert[0], b_group[0], jnp.zeros((LANES - N_EXPERTS - N_GROUPS,), F32)])

    ff = lambda w: jnp.transpose(w[0], (0, 2, 1, 3)).reshape(N_GROUPS, D_MODEL, GROUP_FF).astype(BF16)
    lbs = lb_param.astype(F32)
    return {
        'an': attn_norm[0][None], 'qn': q_norm[0][None], 'kvn': kv_norm[0][None],
        'w_lat': w_in[:, :o_kpe].astype(BF16), 'w_kpe': w_kpe2.astype(BF16),
        'w_hg': w_in[:, o_hg:o_gate].astype(BF16), 'w_gate': w_in[:, o_gate:].astype(BF16),
        'w_uq': w_uq2.astype(BF16), 'w_ukv': w_ukv2.astype(BF16), 'lbs': lbs,
        'hgn': hg_norm[0][None], 'ffn': ffn_norm[0][None],
        'wb0': wb0.reshape(ATT_W, D_MODEL).astype(BF16), 'wb1': w_branch[0, 1].astype(BF16),
        'w_out': w_out[0].astype(BF16), 'w_router': w_router, 'b_router': b_router[None],
        'w_gate_e': ff(w_gate), 'w_up_e': ff(w_up),
        'w_down_e': w_down[0].reshape(N_GROUPS, GROUP_FF, D_MODEL).astype(BF16), 'fn': final_norm[None],
    }


TOKEN_TILE = 512
ATT_TQ = 512
ATT_TK = 512
GLA_ROWS = 256
GLA_CHUNK = 64


def _tile(n, pref):
    t = min(n, pref)
    assert n % t == 0
    return t


def _encoder(x, wts):
    b, l, d = x.shape
    t = b * l
    x2d = x.reshape(t, d)
    tm = _tile(l, TOKEN_TILE)
    q, k, v, hq, hi, fzf, fzb, hg, ga, gb = _in_proj(x2d, l, _rope_tables(l), wts, tm)
    r3 = lambda a: a.reshape(b, l, a.shape[-1])
    oa = _attention(r3(q), r3(k), r3(v), _tile(l, ATT_TQ), _tile(l, ATT_TK))
    of, ob = _gla(r3(hq), r3(hi), r3(fzf), r3(fzb), wts['lbs'], _tile(l, GLA_ROWS), GLA_CHUNK)
    x1, h2, comb = _merge(x2d, oa.reshape(t, ATT_W), of.reshape(t, HG_WIDTH), ob.reshape(t, HG_WIDTH),
                          hg, ga, gb, wts, tm)
    y = _moe(x1, h2, comb, wts, tm)
    return y.reshape(b, l, d)


def kernel(x_prompt, x_sample, attn_norm, w_in, q_norm, w_uq, kv_norm, w_ukv, lb_param, hg_norm, w_branch, w_out,
           ffn_norm, w_group, b_group, w_expert, b_expert, w_gate, w_up, w_down, final_norm):
    wts = _prep_weights(attn_norm, w_in, q_norm, w_uq, kv_norm, w_ukv, lb_param, hg_norm, w_branch, w_out,
                        ffn_norm, w_group, b_group, w_expert, b_expert, w_gate, w_up, w_down, final_norm)
    return (_encoder(x_prompt, wts), _encoder(x_sample, wts))
```

```python
import functools
import math

import jax
import jax.numpy as jnp
from jax import lax
from jax.experimental import pallas as pl
from jax.experimental.pallas import tpu as pltpu

D_MODEL = 1024
MLA_HEADS = 8
Q_LORA = 384
KV_LORA = 256
QK_NOPE = 64
QK_ROPE = 32
V_DIM = 64
QK_DIM = QK_NOPE + QK_ROPE
ROPE_THETA = 10000.0
HG_HEADS = 4
HG_DK = 128
HG_DV = 128
HG_KW = HG_HEADS * HG_DK
HG_WIDTH = HG_HEADS * HG_DV
N_GROUPS = 4
EXPERTS_PER_GROUP = 8
N_EXPERTS = N_GROUPS * EXPERTS_PER_GROUP
D_EXPERT = 256
GROUP_FF = EXPERTS_PER_GROUP * D_EXPERT
EPS = 1e-6

LANES = 128
HEAD_PAD = LANES
ATT_W = MLA_HEADS * HEAD_PAD
VMEM_LIMIT = 56 * 1024 * 1024

F32 = jnp.float32
BF16 = jnp.bfloat16


def _dot(a, b):
    return jnp.dot(a, b, preferred_element_type=F32)


def _dot_nt(a, b):
    return lax.dot_general(a, b, (((1,), (1,)), ((), ())), preferred_element_type=F32)


def _dot_tn(a, b):
    return lax.dot_general(a, b, (((0,), (0,)), ((), ())), preferred_element_type=F32)


def _rms(x, g):
    return x * lax.rsqrt(jnp.mean(x * x, axis=-1, keepdims=True) + EPS) * g


def _const_spec(shape):
    zeros = (0,) * len(shape)
    return pl.BlockSpec(shape, lambda *_: zeros, pipeline_mode=pl.Buffered(1))


def _params(sem):
    return pltpu.CompilerParams(dimension_semantics=sem, vmem_limit_bytes=VMEM_LIMIT)


def _in_proj_kernel(x_ref, cos_ref, sin_ref, an_ref, qn_ref, kvn_ref, w_lat_ref, w_kpe_ref, w_hg_ref,
                    w_gate_ref, w_uq_ref, w_ukv_ref,
                    q_ref, k_ref, v_ref, hq_ref, hi_ref, fzf_ref, fzb_ref, hg_ref, ga_ref, gb_ref):
    h = _rms(x_ref[...], an_ref[...]).astype(BF16)
    cos = cos_ref[...]
    sin = sin_ref[...]

    lat = _dot(h, w_lat_ref[...])
    cqn = _rms(lat[:, :Q_LORA], qn_ref[...]).astype(BF16)
    ckvn = _rms(lat[:, Q_LORA:], kvn_ref[...]).astype(BF16)

    kp = _dot(h, w_kpe_ref[...])
    k_rot = kp[:, :HEAD_PAD] * cos + kp[:, HEAD_PAD:] * sin
    q2 = _dot(cqn, w_uq_ref[...])
    kv2 = _dot(ckvn, w_ukv_ref[...])
    scale = QK_DIM ** -0.5 * math.log2(math.e)
    for hd in range(MLA_HEADS):
        lo, hi = hd * HEAD_PAD, (hd + 1) * HEAD_PAD
        q_ref[:, lo:hi] = ((q2[:, lo:hi] * cos + q2[:, ATT_W + lo:ATT_W + hi] * sin) * scale).astype(BF16)
        k_ref[:, lo:hi] = (kv2[:, lo:hi] + k_rot).astype(BF16)
    lane = lax.broadcasted_iota(jnp.int32, (1, ATT_W), 1)
    v_ref[...] = jnp.where(lane % HEAD_PAD == V_DIM, 1.0, kv2[:, ATT_W:]).astype(BF16)

    hz = _dot(h, w_hg_ref[...])
    hq_ref[...] = (hz[:, :HG_KW] * (HG_DK ** -0.5)).astype(BF16)
    fzf_ref[...] = hz[:, HG_KW:2 * HG_KW]
    fzb_ref[...] = hz[:, 2 * HG_KW:3 * HG_KW]
    hi_ref[...] = hz[:, 3 * HG_KW:3 * HG_KW + HG_WIDTH].astype(BF16)
    hg_ref[...] = hz[:, 3 * HG_KW + HG_WIDTH:].astype(BF16)

    gz = _dot(h, w_gate_ref[...])
    ga_ref[...] = gz[:, :D_MODEL].astype(BF16)
    gb_ref[...] = gz[:, D_MODEL:].astype(BF16)


def _in_proj(x2d, seq_len, tabs, wts, tm):
    t = x2d.shape[0]
    nl = seq_len // tm
    row = lambda w: pl.BlockSpec((tm, w), lambda i: (i, 0))
    tab = pl.BlockSpec((tm, HEAD_PAD), lambda i: (i % nl, 0))
    consts = [wts['an'], wts['qn'], wts['kvn'], wts['w_lat'], wts['w_kpe'], wts['w_hg'], wts['w_gate'],
              wts['w_uq'], wts['w_ukv']]
    out_w = [(ATT_W, BF16), (ATT_W, BF16), (ATT_W, BF16), (HG_KW, BF16), (HG_WIDTH, BF16), (HG_KW, F32),
             (HG_KW, F32), (HG_WIDTH, BF16), (D_MODEL, BF16), (D_MODEL, BF16)]
    return pl.pallas_call(
        _in_proj_kernel,
        name="in_proj",
        grid=(t // tm,),
        in_specs=[row(D_MODEL), tab, tab] + [_const_spec(c.shape) for c in consts],
        out_specs=[row(w) for w, _ in out_w],
        out_shape=[jax.ShapeDtypeStruct((t, w), d) for w, d in out_w],
        compiler_params=_params(("parallel",)),
    )(x2d, tabs[0], tabs[1], *consts)


def _attention_kernel(q_ref, k_ref, v_ref, o_ref, sa_ref, sb_ref, m_ref, acc_ref, *, tk, unroll):
    q = q_ref[0]
    nk = k_ref.shape[1] // tk

    def scores(j):
        off = pl.multiple_of(j * tk, tk)
        return _dot_nt(k_ref[0, pl.ds(off, tk), :], q)

    def absorb(s_ref, j):
        off = pl.multiple_of(j * tk, tk)
        s = s_ref[...]
        m_old = m_ref[...]
        m_new = jnp.maximum(m_old, jnp.max(s, axis=0, keepdims=True))
        p = jnp.exp2(s - m_new).astype(BF16)
        acc_ref[...] = jnp.exp2(m_old - m_new) * acc_ref[...] + _dot_tn(v_ref[0, pl.ds(off, tk), :], p)
        m_ref[...] = m_new

    m_ref[...] = jnp.full(m_ref.shape, -jnp.inf, F32)
    acc_ref[...] = jnp.zeros(acc_ref.shape, F32)
    sa_ref[...] = scores(0)
    bufs = (sa_ref, sb_ref)

    def block(base, last):
        for u in range(unroll):
            if not (last and u == unroll - 1):
                bufs[(u + 1) % 2][...] = scores(base + u + 1)
            absorb(bufs[u % 2], base + u)

    def body(i, carry):
        block(i * unroll, False)
        return carry

    lax.fori_loop(0, nk // unroll - 1, body, 0)
    block(nk - unroll, True)
    acc = acc_ref[...]
    o_ref[0] = (acc / acc[V_DIM:V_DIM + 1, :]).T.astype(o_ref.dtype)


def _attention(q, k, v, tq, tk):
    b, l, _ = q.shape
    nk = l // tk
    unroll = ATT_UNROLL if nk >= 3 * ATT_UNROLL else 2
    assert nk % unroll == 0
    qspec = pl.BlockSpec((1, tq, HEAD_PAD), lambda bi, h, i: (bi, i, h))
    kspec = pl.BlockSpec((1, l, HEAD_PAD), lambda bi, h, i: (bi, 0, h))
    return pl.pallas_call(
        functools.partial(_attention_kernel, tk=tk, unroll=unroll),
        name="attention",
        grid=(b, MLA_HEADS, l // tq),
        in_specs=[qspec, kspec, kspec],
        out_specs=qspec,
        out_shape=jax.ShapeDtypeStruct((b, l, ATT_W), BF16),
        scratch_shapes=[pltpu.VMEM((tk, tq), F32), pltpu.VMEM((tk, tq), F32), pltpu.VMEM((1, tq), F32),
                        pltpu.VMEM((HEAD_PAD, tq), F32)],
        compiler_params=_params(("parallel", "parallel", "arbitrary")),
    )(q, k, v)


def _gla_masks(c, reverse):
    t = lax.broadcasted_iota(jnp.int32, (c, HG_DK), 0)
    tau = (c - 1 - t) if reverse else t
    uppers = [((tau >> j) & 1) == 1 for j in range(int(math.log2(c)))]
    row = lax.broadcasted_iota(jnp.int32, (c, c), 0)
    col = lax.broadcasted_iota(jnp.int32, (c, c), 1)
    lev = 31 - lax.clz(row ^ col)
    lev = jnp.where((row <= col) if reverse else (row >= col), lev, -2)
    return uppers, lev


def _gla_chunk(q, v, z, lb, st_ref, masks, reverse):
    uppers, lev = masks
    c = q.shape[0]
    q = q.astype(F32)
    f = lb + (1.0 - lb) * jax.nn.sigmoid(z)
    kk = 1.0 - f

    def from_prev(a, m):
        return pltpu.roll(a, (c - m) if reverse else m, 0)

    def from_next(a, m):
        return pltpu.roll(a, m if reverse else (c - m), 0)

    kk16 = kk.astype(BF16)
    a_mat = jnp.where(lev == -1, _dot_nt(q.astype(BF16), kk16), 0.0)
    head = f
    tail = None
    tot = f
    for j, upper in enumerate(uppers):
        m = 1 << j
        kj = kk16 if tail is None else (kk * tail).astype(BF16)
        a_mat = jnp.where(lev == j, _dot_nt((q * head).astype(BF16), kj), a_mat)
        sib = jnp.where(upper, from_prev(tot, m), from_next(tot, m))
        head = jnp.where(upper, head * sib, head)
        tail = jnp.where(upper, 1.0, sib) if tail is None else jnp.where(upper, tail, tail * sib)
        tot = tot * sib

    st = st_ref[...]
    o = _dot(a_mat.astype(BF16), v) + _dot_nt((q * head).astype(BF16), st.astype(BF16))
    st_ref[...] = st * tot[0:1, :] + _dot_tn(v, (kk * tail).astype(BF16))
    return o


def _gla_kernel(qf_ref, vf_ref, zf_ref, qb_ref, vb_ref, zb_ref, lb_ref, of_ref, ob_ref, sf_ref, sb_ref, *, chunk):
    @pl.when(pl.program_id(2) == 0)
    def _():
        sf_ref[...] = jnp.zeros_like(sf_ref)
        sb_ref[...] = jnp.zeros_like(sb_ref)

    tb = qf_ref.shape[1]
    n = tb // chunk
    lbp = lb_ref[...]
    lbe = jnp.exp(lbp - jnp.max(lbp, axis=0, keepdims=True))
    lb = lbe[0] / jnp.sum(lbe, axis=0)
    lbf = lb[0:1, :]
    lbb = lb[1:2, :]
    masks_f = _gla_masks(chunk, False)
    masks_b = _gla_masks(chunk, True)
    for c in range(n):
        sl = pl.ds(c * chunk, chunk)
        of_ref[0, sl, :] = _gla_chunk(qf_ref[0, sl, :], vf_ref[0, sl, :], zf_ref[0, sl, :], lbf, sf_ref,
                                      masks_f, False)
        sl = pl.ds((n - 1 - c) * chunk, chunk)
        ob_ref[0, sl, :] = _gla_chunk(qb_ref[0, sl, :], vb_ref[0, sl, :], zb_ref[0, sl, :], lbb, sb_ref,
                                      masks_b, True)


def _gla(hq, hi, fzf, fzb, lbs, tb, chunk):
    b, l, _ = hq.shape
    nt = l // tb
    fwd = pl.BlockSpec((1, tb, HG_DK), lambda bi, h, i: (bi, i, h))
    bwd = pl.BlockSpec((1, tb, HG_DK), lambda bi, h, i: (bi, nt - 1 - i, h))
    lbspec = pl.BlockSpec((lbs.shape[0], 2, HG_DK), lambda bi, h, i: (0, 0, h))
    out = jax.ShapeDtypeStruct((b, l, HG_WIDTH), F32)
    return pl.pallas_call(
        functools.partial(_gla_kernel, chunk=chunk),
        name="gla",
        grid=(b, HG_HEADS, nt),
        in_specs=[fwd, fwd, fwd, bwd, bwd, bwd, lbspec],
        out_specs=[fwd, bwd],
        out_shape=[out, out],
        scratch_shapes=[pltpu.VMEM((HG_DV, HG_DK), F32), pltpu.VMEM((HG_DV, HG_DK), F32)],
        compiler_params=_params(("parallel", "parallel", "arbitrary")),
    )(hq, hi, fzf, hq, hi, fzb, lbs)


def _route(logits):
    lane = lax.broadcasted_iota(jnp.int32, logits.shape, 1)
    neg = -jnp.inf
    big = jnp.int32(LANES)

    is_g = (lane >= N_EXPERTS) & (lane < N_EXPERTS + N_GROUPS)
    gl = jnp.where(is_g, logits, neg)
    ge = jnp.exp(gl - jnp.max(gl, axis=-1, keepdims=True))
    gp = ge / jnp.sum(ge, axis=-1, keepdims=True)
    g_val = jnp.max(gp, axis=-1, keepdims=True)
    g_idx = jnp.min(jnp.where(is_g & (gp == g_val), lane, big), axis=-1, keepdims=True) - N_EXPERTS

    sel = (lane < N_EXPERTS) & ((lane >> 3) == g_idx)
    el = jnp.where(sel, logits, neg)
    ee = jnp.exp(el - jnp.max(el, axis=-1, keepdims=True))
    ep = ee / jnp.sum(ee, axis=-1, keepdims=True)
    v1 = jnp.max(ep, axis=-1, keepdims=True)
    i1 = jnp.min(jnp.where(sel & (ep == v1), lane, big), axis=-1, keepdims=True)
    rest = jnp.where(sel & (lane != i1), ep, -1.0)
    v2 = jnp.max(rest, axis=-1, keepdims=True)
    i2 = jnp.min(jnp.where(rest == v2, lane, big), axis=-1, keepdims=True)
    den = v1 + v2
    return jnp.where(lane == i1, g_val * v1 / den, 0.0) + jnp.where(lane == i2, g_val * v2 / den, 0.0)


def _merge_kernel(x_ref, oa_ref, of_ref, ob_ref, hg_ref, ga_ref, gb_ref, hgn_ref, ffn_ref, wb0_ref, wb1_ref,
                  wout_ref, wr_ref, br_ref, x1_ref, h2_ref, comb_ref):
    o = of_ref[...] + ob_ref[...]
    hgn = hgn_ref[...]
    parts = []
    for hd in range(HG_HEADS):
        lo, hi = hd * HG_DV, (hd + 1) * HG_DV
        parts.append(_rms(o[:, lo:hi], hgn[:, lo:hi]))
    o_b = (jnp.concatenate(parts, axis=-1) * jax.nn.silu(hg_ref[...].astype(F32))).astype(BF16)
    merged = (jax.nn.sigmoid(ga_ref[...].astype(F32)) * _dot(oa_ref[...], wb0_ref[...])
              + jax.nn.sigmoid(gb_ref[...].astype(F32)) * _dot(o_b, wb1_ref[...]))
    x1 = x_ref[...] + _dot(merged.astype(BF16), wout_ref[...])
    x1_ref[...] = x1
    h2 = _rms(x1, ffn_ref[...])
    h2_hi = h2.astype(BF16)
    h2_ref[...] = h2_hi
    h2_lo = (h2 - h2_hi.astype(F32)).astype(BF16)
    r = _dot(h2_hi, wr_ref[...])
    logits = r[:, :LANES] + r[:, LANES:] + _dot(h2_lo, wr_ref[:, :LANES]) + br_ref[...]
    comb_ref[...] = _route(logits)


def _merge(x2d, oa, of, ob, hg, ga, gb, wts, tm):
    t = x2d.shape[0]
    row = lambda w: pl.BlockSpec((tm, w), lambda i: (i, 0))
    consts = [wts['hgn'], wts['ffn'], wts['wb0'], wts['wb1'], wts['w_out'], wts['w_router'], wts['b_router']]
    return pl.pallas_call(
        _merge_kernel,
        name="merge",
        grid=(t // tm,),
        in_specs=[row(D_MODEL), row(ATT_W), row(HG_WIDTH), row(HG_WIDTH), row(HG_WIDTH), row(D_MODEL),
                  row(D_MODEL)] + [_const_spec(c.shape) for c in consts],
        out_specs=[row(D_MODEL), row(D_MODEL), row(LANES)],
        out_shape=[jax.ShapeDtypeStruct((t, D_MODEL), F32), jax.ShapeDtypeStruct((t, D_MODEL), BF16),
                   jax.ShapeDtypeStruct((t, LANES), F32)],
        compiler_params=_params(("parallel",)),
    )(x2d, oa, of, ob, hg, ga, gb, *consts)


def _moe_kernel(x1_ref, h2_ref, comb_ref, wg_ref, wu_ref, wd_ref, fn_ref, y_ref, acc_ref, hh_ref):
    grp = pl.program_id(1)

    @pl.when(grp == 0)
    def _():
        acc_ref[...] = jnp.zeros_like(acc_ref)

    h2 = h2_ref[...]
    gate = _dot(h2, wg_ref[0])
    up = _dot(h2, wu_ref[0])
    comb = comb_ref[...]
    lane = lax.broadcasted_iota(jnp.int32, comb.shape, 1)
    for e in range(EXPERTS_PER_GROUP):
        lo, hi = e * D_EXPERT, (e + 1) * D_EXPERT
        w_e = jnp.sum(jnp.where(lane == grp * EXPERTS_PER_GROUP + e, comb, 0.0), axis=-1, keepdims=True)
        hh_ref[:, lo:hi] = (jax.nn.silu(gate[:, lo:hi]) * up[:, lo:hi] * w_e).astype(BF16)
    acc_ref[...] += _dot(hh_ref[...], wd_ref[0])

    @pl.when(grp == N_GROUPS - 1)
    def _():
        y_ref[...] = _rms(x1_ref[...] + acc_ref[...], fn_ref[...])


def _moe(x1, h2, comb, wts, tm):
    t = x1.shape[0]
    row = lambda w: pl.BlockSpec((tm, w), lambda i, g: (i, 0))
    return pl.pallas_call(
        _moe_kernel,
        name="moe",
        grid=(t // tm, N_GROUPS),
        in_specs=[row(D_MODEL), row(D_MODEL), row(LANES),
                  pl.BlockSpec((1, D_MODEL, GROUP_FF), lambda i, g: (g, 0, 0)),
                  pl.BlockSpec((1, D_MODEL, GROUP_FF), lambda i, g: (g, 0, 0)),
                  pl.BlockSpec((1, GROUP_FF, D_MODEL), lambda i, g: (g, 0, 0)),
                  pl.BlockSpec((1, D_MODEL), lambda i, g: (0, 0))],
        out_specs=row(D_MODEL),
        out_shape=jax.ShapeDtypeStruct((t, D_MODEL), F32),
        scratch_shapes=[pltpu.VMEM((tm, D_MODEL), F32), pltpu.VMEM((tm, GROUP_FF), BF16)],
        compiler_params=_params(("parallel", "arbitrary")),
    )(x1, h2, comb, wts['w_gate_e'], wts['w_up_e'], wts['w_down_e'], wts['fn'])


def _rope_tables(seq_len):
    pos = jnp.arange(seq_len, dtype=F32)
    inv_freq = 1.0 / (ROPE_THETA ** (jnp.arange(0, QK_ROPE, 2, dtype=F32) / QK_ROPE))
    ang = pos[:, None] * inv_freq[None, :]
    cos, sin = jnp.cos(ang), jnp.sin(ang)
    pad = jnp.zeros((seq_len, HEAD_PAD - QK_DIM), F32)
    cos_t = jnp.concatenate([jnp.ones((seq_len, QK_NOPE), F32), cos, cos, pad], axis=-1)
    sin_t = jnp.concatenate([jnp.zeros((seq_len, QK_NOPE), F32), sin, sin, pad], axis=-1)
    return cos_t, sin_t


def _rot_half_cols(w):
    half = QK_ROPE // 2
    return jnp.concatenate([-w[..., half:], w[..., :half]], axis=-1)


def _prep_weights(attn_norm, w_in, q_norm, w_uq, kv_norm, w_ukv, lb_param, hg_norm, w_branch, w_out, ffn_norm,
                  w_group, b_group, w_expert, b_expert, w_gate, w_up, w_down, final_norm):
    w_in = w_in[0]
    o_kpe = Q_LORA + KV_LORA
    o_hg = o_kpe + QK_ROPE
    o_gate = o_hg + 3 * HG_KW + 2 * HG_WIDTH
    w_kpe = w_in[:, o_kpe:o_hg]
    zpad = lambda n: jnp.zeros((D_MODEL, n), F32)
    place = lambda w: jnp.concatenate([zpad(QK_NOPE), w, zpad(HEAD_PAD - QK_DIM)], axis=-1)
    w_kpe2 = jnp.concatenate([place(w_kpe), place(_rot_half_cols(w_kpe))], axis=-1)

    uq = w_uq[0]
    zq = lambda n: jnp.zeros((Q_LORA, MLA_HEADS, n), F32)
    uq_full = jnp.concatenate([uq, zq(HEAD_PAD - QK_DIM)], axis=-1)
    uq_rot = jnp.concatenate([zq(QK_NOPE), _rot_half_cols(uq[..., QK_NOPE:]), zq(HEAD_PAD - QK_DIM)], axis=-1)
    w_uq2 = jnp.concatenate([uq_full.reshape(Q_LORA, ATT_W), uq_rot.reshape(Q_LORA, ATT_W)], axis=-1)

    ukv = w_ukv[0]
    zk = lambda n: jnp.zeros((KV_LORA, MLA_HEADS, n), F32)
    uk = jnp.concatenate([ukv[..., :QK_NOPE], zk(HEAD_PAD - QK_NOPE)], axis=-1)
    uv = jnp.concatenate([ukv[..., QK_NOPE:], zk(HEAD_PAD - V_DIM)], axis=-1)
    w_ukv2 = jnp.concatenate([uk.reshape(KV_LORA, ATT_W), uv.reshape(KV_LORA, ATT_W)], axis=-1)

    wb0 = w_branch[0, 0].reshape(MLA_HEADS, V_DIM, D_MODEL)
    wb0 = jnp.concatenate([wb0, jnp.zeros((MLA_HEADS, HEAD_PAD - V_DIM, D_MODEL), F32)], axis=1)

    w_router = jnp.concatenate([w_expert[0], w_group[0],
                                jnp.zeros((D_MODEL, LANES - N_EXPERTS - N_GROUPS), F32)], axis=-1)
    b_router = jnp.concatenate([b_expert[0], b_group[0], jnp.zeros((LANES - N_EXPERTS - N_GROUPS,), F32)])
    w_router_hi = w_router.astype(BF16)
    w_router_lo = (w_router - w_router_hi.astype(F32)).astype(BF16)
    w_router = jnp.concatenate([w_router_hi, w_router_lo], axis=-1)

    ff = lambda w: jnp.transpose(w[0], (0, 2, 1, 3)).reshape(N_GROUPS, D_MODEL, GROUP_FF).astype(BF16)
    lbs = lb_param.astype(F32)
    return {
        'an': attn_norm[0][None], 'qn': q_norm[0][None], 'kvn': kv_norm[0][None],
        'w_lat': w_in[:, :o_kpe].astype(BF16), 'w_kpe': w_kpe2.astype(BF16),
        'w_hg': w_in[:, o_hg:o_gate].astype(BF16), 'w_gate': w_in[:, o_gate:].astype(BF16),
        'w_uq': w_uq2.astype(BF16), 'w_ukv': w_ukv2.astype(BF16), 'lbs': lbs,
        'hgn': hg_norm[0][None], 'ffn': ffn_norm[0][None],
        'wb0': wb0.reshape(ATT_W, D_MODEL).astype(BF16), 'wb1': w_branch[0, 1].astype(BF16),
        'w_out': w_out[0].astype(BF16), 'w_router': w_router, 'b_router': b_router[None],
        'w_gate_e': ff(w_gate), 'w_up_e': ff(w_up),
        'w_down_e': w_down[0].reshape(N_GROUPS, GROUP_FF, D_MODEL).astype(BF16), 'fn': final_norm[None],
    }


TOKEN_TILE = 512
ATT_TQ = 512
ATT_TK = 512
ATT_UNROLL = 4
GLA_ROWS = 256
GLA_CHUNK = 64


def _tile(n, pref):
    t = min(n, pref)
    assert n % t == 0
    return t


def _encoder(x, wts):
    b, l, d = x.shape
    t = b * l
    x2d = x.reshape(t, d)
    tm = _tile(l, TOKEN_TILE)
    q, k, v, hq, hi, fzf, fzb, hg, ga, gb = _in_proj(x2d, l, _rope_tables(l), wts, tm)
    r3 = lambda a: a.reshape(b, l, a.shape[-1])
    oa = _attention(r3(q), r3(k), r3(v), _tile(l, ATT_TQ), _tile(l, ATT_TK))
    of, ob = _gla(r3(hq), r3(hi), r3(fzf), r3(fzb), wts['lbs'], _tile(l, GLA_ROWS), GLA_CHUNK)
    x1, h2, comb = _merge(x2d, oa.reshape(t, ATT_W), of.reshape(t, HG_WIDTH), ob.reshape(t, HG_WIDTH),
                          hg, ga, gb, wts, tm)
    y = _moe(x1, h2, comb, wts, tm)
    return y.reshape(b, l, d)


def kernel(x_prompt, x_sample, attn_norm, w_in, q_norm, w_uq, kv_norm, w_ukv, lb_param, hg_norm, w_branch, w_out,
           ffn_norm, w_group, b_group, w_expert, b_expert, w_gate, w_up, w_down, final_norm):
    wts = _prep_weights(attn_norm, w_in, q_norm, w_uq, kv_norm, w_ukv, lb_param, hg_norm, w_branch, w_out,
                        ffn_norm, w_group, b_group, w_expert, b_expert, w_gate, w_up, w_down, final_norm)
    return (_encoder(x_prompt, wts), _encoder(x_sample, wts))
```

```python
import functools
import math

import jax
import jax.numpy as jnp
from jax import lax
from jax.experimental import pallas as pl
from jax.experimental.pallas import tpu as pltpu

D_MODEL = 1024
MLA_HEADS = 8
Q_LORA = 384
KV_LORA = 256
QK_NOPE = 64
QK_ROPE = 32
V_DIM = 64
QK_DIM = QK_NOPE + QK_ROPE
ROPE_THETA = 10000.0
HG_HEADS = 4
HG_DK = 128
HG_DV = 128
HG_KW = HG_HEADS * HG_DK
HG_WIDTH = HG_HEADS * HG_DV
N_GROUPS = 4
EXPERTS_PER_GROUP = 8
N_EXPERTS = N_GROUPS * EXPERTS_PER_GROUP
D_EXPERT = 256
GROUP_FF = EXPERTS_PER_GROUP * D_EXPERT
EPS = 1e-6

LANES = 128
HEAD_PAD = LANES
ATT_W = MLA_HEADS * HEAD_PAD
VMEM_LIMIT = 56 * 1024 * 1024

F32 = jnp.float32
BF16 = jnp.bfloat16


def _dot(a, b):
    return jnp.dot(a, b, preferred_element_type=F32)


def _dot_nt(a, b):
    return lax.dot_general(a, b, (((1,), (1,)), ((), ())), preferred_element_type=F32)


def _dot_tn(a, b):
    return lax.dot_general(a, b, (((0,), (0,)), ((), ())), preferred_element_type=F32)


def _rms(x, g):
    return x * lax.rsqrt(jnp.mean(x * x, axis=-1, keepdims=True) + EPS) * g


def _const_spec(shape):
    zeros = (0,) * len(shape)
    return pl.BlockSpec(shape, lambda *_: zeros, pipeline_mode=pl.Buffered(1))


def _params(sem):
    return pltpu.CompilerParams(dimension_semantics=sem, vmem_limit_bytes=VMEM_LIMIT)


def _in_proj_kernel(x_ref, cos_ref, sin_ref, an_ref, qn_ref, kvn_ref, w_lat_ref, w_kpe_ref, w_hg_ref,
                    w_gate_ref, w_uq_ref, w_ukv_ref,
                    q_ref, k_ref, v_ref, hq_ref, hi_ref, fzf_ref, fzb_ref, hg_ref, ga_ref, gb_ref):
    h = _rms(x_ref[...], an_ref[...]).astype(BF16)
    cos = cos_ref[...]
    sin = sin_ref[...]

    lat = _dot(h, w_lat_ref[...])
    cqn = _rms(lat[:, :Q_LORA], qn_ref[...]).astype(BF16)
    ckvn = _rms(lat[:, Q_LORA:], kvn_ref[...]).astype(BF16)

    kp = _dot(h, w_kpe_ref[...])
    k_rot = kp[:, :HEAD_PAD] * cos + kp[:, HEAD_PAD:] * sin
    q2 = _dot(cqn, w_uq_ref[...])
    kv2 = _dot(ckvn, w_ukv_ref[...])
    scale = QK_DIM ** -0.5 * math.log2(math.e)
    for hd in range(MLA_HEADS):
        lo, hi = hd * HEAD_PAD, (hd + 1) * HEAD_PAD
        q_ref[:, lo:hi] = ((q2[:, lo:hi] * cos + q2[:, ATT_W + lo:ATT_W + hi] * sin) * scale).astype(BF16)
        k_ref[:, lo:hi] = (kv2[:, lo:hi] + k_rot).astype(BF16)
    lane = lax.broadcasted_iota(jnp.int32, (1, ATT_W), 1)
    v_ref[...] = jnp.where(lane % HEAD_PAD == V_DIM, 1.0, kv2[:, ATT_W:]).astype(BF16)

    hz = _dot(h, w_hg_ref[...])
    hq_ref[...] = (hz[:, :HG_KW] * (HG_DK ** -0.5)).astype(BF16)
    fzf_ref[...] = hz[:, HG_KW:2 * HG_KW]
    fzb_ref[...] = hz[:, 2 * HG_KW:3 * HG_KW]
    hi_ref[...] = hz[:, 3 * HG_KW:3 * HG_KW + HG_WIDTH].astype(BF16)
    hg_ref[...] = hz[:, 3 * HG_KW + HG_WIDTH:].astype(BF16)

    gz = _dot(h, w_gate_ref[...])
    ga_ref[...] = gz[:, :D_MODEL].astype(BF16)
    gb_ref[...] = gz[:, D_MODEL:].astype(BF16)


def _in_proj(x2d, seq_len, tabs, wts, tm):
    t = x2d.shape[0]
    nl = seq_len // tm
    row = lambda w: pl.BlockSpec((tm, w), lambda i: (i, 0))
    tab = pl.BlockSpec((tm, HEAD_PAD), lambda i: (i % nl, 0))
    consts = [wts['an'], wts['qn'], wts['kvn'], wts['w_lat'], wts['w_kpe'], wts['w_hg'], wts['w_gate'],
              wts['w_uq'], wts['w_ukv']]
    out_w = [(ATT_W, BF16), (ATT_W, BF16), (ATT_W, BF16), (HG_KW, BF16), (HG_WIDTH, BF16), (HG_KW, F32),
             (HG_KW, F32), (HG_WIDTH, BF16), (D_MODEL, BF16), (D_MODEL, BF16)]
    return pl.pallas_call(
        _in_proj_kernel,
        name="in_proj",
        grid=(t // tm,),
        in_specs=[row(D_MODEL), tab, tab] + [_const_spec(c.shape) for c in consts],
        out_specs=[row(w) for w, _ in out_w],
        out_shape=[jax.ShapeDtypeStruct((t, w), d) for w, d in out_w],
        compiler_params=_params(("parallel",)),
    )(x2d, tabs[0], tabs[1], *consts)


def _attention_kernel(q_ref, k_ref, v_ref, o_ref, sa_ref, sb_ref, m_ref, acc_ref, *, tk, unroll):
    q = q_ref[0]
    nk = k_ref.shape[1] // tk

    def scores(j):
        off = pl.multiple_of(j * tk, tk)
        return _dot_nt(k_ref[0, pl.ds(off, tk), :], q)

    def absorb(s_ref, j):
        off = pl.multiple_of(j * tk, tk)
        s = s_ref[...]
        m_old = m_ref[...]
        m_new = jnp.maximum(m_old, jnp.max(s, axis=0, keepdims=True))
        p = jnp.exp2(s - m_new).astype(BF16)
        acc_ref[...] = jnp.exp2(m_old - m_new) * acc_ref[...] + _dot_tn(v_ref[0, pl.ds(off, tk), :], p)
        m_ref[...] = m_new

    m_ref[...] = jnp.full(m_ref.shape, -jnp.inf, F32)
    acc_ref[...] = jnp.zeros(acc_ref.shape, F32)
    sa_ref[...] = scores(0)
    bufs = (sa_ref, sb_ref)

    def block(base, last):
        for u in range(unroll):
            if not (last and u == unroll - 1):
                bufs[(u + 1) % 2][...] = scores(base + u + 1)
            absorb(bufs[u % 2], base + u)

    def body(i, carry):
        block(i * unroll, False)
        return carry

    lax.fori_loop(0, nk // unroll - 1, body, 0)
    block(nk - unroll, True)
    acc = acc_ref[...]
    o_ref[0] = (acc / acc[V_DIM:V_DIM + 1, :]).T.astype(o_ref.dtype)


def _attention(q, k, v, tq, tk):
    b, l, _ = q.shape
    nk = l // tk
    unroll = ATT_UNROLL if nk >= 3 * ATT_UNROLL else 2
    assert nk % unroll == 0
    qspec = pl.BlockSpec((1, tq, HEAD_PAD), lambda bi, h, i: (bi, i, h))
    kspec = pl.BlockSpec((1, l, HEAD_PAD), lambda bi, h, i: (bi, 0, h))
    return pl.pallas_call(
        functools.partial(_attention_kernel, tk=tk, unroll=unroll),
        name="attention",
        grid=(b, MLA_HEADS, l // tq),
        in_specs=[qspec, kspec, kspec],
        out_specs=qspec,
        out_shape=jax.ShapeDtypeStruct((b, l, ATT_W), BF16),
        scratch_shapes=[pltpu.VMEM((tk, tq), F32), pltpu.VMEM((tk, tq), F32), pltpu.VMEM((1, tq), F32),
                        pltpu.VMEM((HEAD_PAD, tq), F32)],
        compiler_params=_params(("parallel", "parallel", "arbitrary")),
    )(q, k, v)


def _gla_masks(c, reverse):
    t = lax.broadcasted_iota(jnp.int32, (c, HG_DK), 0)
    tau = (c - 1 - t) if reverse else t
    uppers = [((tau >> j) & 1) == 1 for j in range(int(math.log2(c)))]
    row = lax.broadcasted_iota(jnp.int32, (c, c), 0)
    col = lax.broadcasted_iota(jnp.int32, (c, c), 1)
    lev = 31 - lax.clz(row ^ col)
    lev = jnp.where((row <= col) if reverse else (row >= col), lev, -2)
    return uppers, lev


def _gla_chunk(q, v, z, lb, st_ref, masks, reverse):
    uppers, lev = masks
    c = q.shape[0]
    q = q.astype(F32)
    f = lb + (1.0 - lb) * jax.nn.sigmoid(z)
    kk = 1.0 - f

    def from_prev(a, m):
        return pltpu.roll(a, (c - m) if reverse else m, 0)

    def from_next(a, m):
        return pltpu.roll(a, m if reverse else (c - m), 0)

    kk16 = kk.astype(BF16)
    a_mat = jnp.where(lev == -1, _dot_nt(q.astype(BF16), kk16), 0.0)
    head = f
    tail = None
    tot = f
    for j, upper in enumerate(uppers):
        m = 1 << j
        kj = kk16 if tail is None else (kk * tail).astype(BF16)
        a_mat = jnp.where(lev == j, _dot_nt((q * head).astype(BF16), kj), a_mat)
        sib = jnp.where(upper, from_prev(tot, m), from_next(tot, m))
        head = jnp.where(upper, head * sib, head)
        tail = jnp.where(upper, 1.0, sib) if tail is None else jnp.where(upper, tail, tail * sib)
        tot = tot * sib

    st = st_ref[...]
    o = _dot(a_mat.astype(BF16), v) + _dot_nt((q * head).astype(BF16), st.astype(BF16))
    st_ref[...] = st * tot[0:1, :] + _dot_tn(v, (kk * tail).astype(BF16))
    return o


def _gla_kernel(qf_ref, vf_ref, zf_ref, qb_ref, vb_ref, zb_ref, lb_ref, of_ref, ob_ref, sf_ref, sb_ref, *, chunk):
    @pl.when(pl.program_id(2) == 0)
    def _():
        sf_ref[...] = jnp.zeros_like(sf_ref)
        sb_ref[...] = jnp.zeros_like(sb_ref)

    tb = qf_ref.shape[1]
    n = tb // chunk
    lbp = lb_ref[...]
    lbe = jnp.exp(lbp - jnp.max(lbp, axis=0, keepdims=True))
    lb = lbe[0] / jnp.sum(lbe, axis=0)
    lbf = lb[0:1, :]
    lbb = lb[1:2, :]
    masks_f = _gla_masks(chunk, False)
    masks_b = _gla_masks(chunk, True)
    for c in range(n):
        sl = pl.ds(c * chunk, chunk)
        of_ref[0, sl, :] = _gla_chunk(qf_ref[0, sl, :], vf_ref[0, sl, :], zf_ref[0, sl, :], lbf, sf_ref,
                                      masks_f, False)
        sl = pl.ds((n - 1 - c) * chunk, chunk)
        ob_ref[0, sl, :] = _gla_chunk(qb_ref[0, sl, :], vb_ref[0, sl, :], zb_ref[0, sl, :], lbb, sb_ref,
                                      masks_b, True)


def _gla(hq, hi, fzf, fzb, lbs, tb, chunk):
    b, l, _ = hq.shape
    nt = l // tb
    fwd = pl.BlockSpec((1, tb, HG_DK), lambda bi, h, i: (bi, i, h))
    bwd = pl.BlockSpec((1, tb, HG_DK), lambda bi, h, i: (bi, nt - 1 - i, h))
    lbspec = pl.BlockSpec((lbs.shape[0], 2, HG_DK), lambda bi, h, i: (0, 0, h))
    out = jax.ShapeDtypeStruct((b, l, HG_WIDTH), F32)
    return pl.pallas_call(
        functools.partial(_gla_kernel, chunk=chunk),
        name="gla",
        grid=(b, HG_HEADS, nt),
        in_specs=[fwd, fwd, fwd, bwd, bwd, bwd, lbspec],
        out_specs=[fwd, bwd],
        out_shape=[out, out],
        scratch_shapes=[pltpu.VMEM((HG_DV, HG_DK), F32), pltpu.VMEM((HG_DV, HG_DK), F32)],
        compiler_params=_params(("parallel", "parallel", "arbitrary")),
    )(hq, hi, fzf, hq, hi, fzb, lbs)


def _route(logits):
    lane = lax.broadcasted_iota(jnp.int32, logits.shape, 1)
    neg = -jnp.inf
    big = jnp.int32(LANES)

    is_g = (lane >= N_EXPERTS) & (lane < N_EXPERTS + N_GROUPS)
    gl = jnp.where(is_g, logits, neg)
    ge = jnp.exp(gl - jnp.max(gl, axis=-1, keepdims=True))
    gp = ge / jnp.sum(ge, axis=-1, keepdims=True)
    g_val = jnp.max(gp, axis=-1, keepdims=True)
    g_idx = jnp.min(jnp.where(is_g & (gp == g_val), lane, big), axis=-1, keepdims=True) - N_EXPERTS

    sel = (lane < N_EXPERTS) & ((lane >> 3) == g_idx)
    el = jnp.where(sel, logits, neg)
    ee = jnp.exp(el - jnp.max(el, axis=-1, keepdims=True))
    ep = ee / jnp.sum(ee, axis=-1, keepdims=True)
    v1 = jnp.max(ep, axis=-1, keepdims=True)
    i1 = jnp.min(jnp.where(sel & (ep == v1), lane, big), axis=-1, keepdims=True)
    rest = jnp.where(sel & (lane != i1), ep, -1.0)
    v2 = jnp.max(rest, axis=-1, keepdims=True)
    i2 = jnp.min(jnp.where(rest == v2, lane, big), axis=-1, keepdims=True)
    den = v1 + v2
    comb = jnp.where(lane == i1, g_val * v1 / den, 0.0) + jnp.where(lane == i2, g_val * v2 / den, 0.0)
    return comb + jnp.where(lane == N_EXPERTS, g_idx.astype(F32), 0.0)


def _merge_kernel(x_ref, oa_ref, of_ref, ob_ref, hg_ref, ga_ref, gb_ref, hgn_ref, ffn_ref, wb0_ref, wb1_ref,
                  wout_ref, wr_ref, br_ref, x1_ref, h2_ref, comb_ref):
    o = of_ref[...] + ob_ref[...]
    hgn = hgn_ref[...]
    parts = []
    for hd in range(HG_HEADS):
        lo, hi = hd * HG_DV, (hd + 1) * HG_DV
        parts.append(_rms(o[:, lo:hi], hgn[:, lo:hi]))
    o_b = (jnp.concatenate(parts, axis=-1) * jax.nn.silu(hg_ref[...].astype(F32))).astype(BF16)
    merged = (jax.nn.sigmoid(ga_ref[...].astype(F32)) * _dot(oa_ref[...], wb0_ref[...])
              + jax.nn.sigmoid(gb_ref[...].astype(F32)) * _dot(o_b, wb1_ref[...]))
    x1 = x_ref[...] + _dot(merged.astype(BF16), wout_ref[...])
    x1_ref[...] = x1
    h2 = _rms(x1, ffn_ref[...])
    h2_hi = h2.astype(BF16)
    h2_ref[...] = h2_hi
    h2_lo = (h2 - h2_hi.astype(F32)).astype(BF16)
    r = _dot(h2_hi, wr_ref[...])
    logits = r[:, :LANES] + r[:, LANES:] + _dot(h2_lo, wr_ref[:, :LANES]) + br_ref[...]
    comb_ref[...] = _route(logits)


def _merge(x2d, oa, of, ob, hg, ga, gb, wts, tm):
    t = x2d.shape[0]
    row = lambda w: pl.BlockSpec((tm, w), lambda i: (i, 0))
    consts = [wts['hgn'], wts['ffn'], wts['wb0'], wts['wb1'], wts['w_out'], wts['w_router'], wts['b_router']]
    return pl.pallas_call(
        _merge_kernel,
        name="merge",
        grid=(t // tm,),
        in_specs=[row(D_MODEL), row(ATT_W), row(HG_WIDTH), row(HG_WIDTH), row(HG_WIDTH), row(D_MODEL),
                  row(D_MODEL)] + [_const_spec(c.shape) for c in consts],
        out_specs=[row(D_MODEL), row(D_MODEL), row(LANES)],
        out_shape=[jax.ShapeDtypeStruct((t, D_MODEL), F32), jax.ShapeDtypeStruct((t, D_MODEL), BF16),
                   jax.ShapeDtypeStruct((t, LANES), F32)],
        compiler_params=_params(("parallel",)),
    )(x2d, oa, of, ob, hg, ga, gb, *consts)


def _moe_kernel(x1_ref, h2_ref, comb_ref, wg_ref, wu_ref, wd_ref, fn_ref, y_ref,
                acc_ref, pos_ref, cw_ref, tri_ref, hh_ref, nblk_ref, *, blk):
    tile = pl.program_id(0)
    grp = pl.program_id(1)
    tm = h2_ref.shape[0]

    @pl.when((tile == 0) & (grp == 0))
    def _():
        r = lax.broadcasted_iota(jnp.int32, (tm, tm), 0)
        c = lax.broadcasted_iota(jnp.int32, (tm, tm), 1)
        tri_ref[...] = jnp.where(r < c, 1.0, 0.0).astype(BF16)

    @pl.when(grp == 0)
    def _():
        acc_ref[...] = jnp.zeros_like(acc_ref)
        comb = comb_ref[...]
        hi = comb.astype(BF16)
        cw_ref[:, :LANES] = hi
        cw_ref[:, LANES:] = (comb - hi.astype(F32)).astype(BF16)
        gid = comb.T[N_EXPERTS:N_EXPERTS + 1, :].astype(jnp.int32)
        g8 = lax.broadcasted_iota(jnp.int32, (8, tm), 0)
        onehot = g8 == gid
        ones = jnp.where(onehot, 1.0, 0.0)
        rank = _dot(ones.astype(BF16), tri_ref[...])
        nblk = (jnp.sum(ones, axis=1, keepdims=True).astype(jnp.int32) + (blk - 1)) >> int(math.log2(blk))
        start = jnp.zeros((8, 1), jnp.int32)
        run = jnp.zeros((1, 1), jnp.int32)
        g81 = lax.broadcasted_iota(jnp.int32, (8, 1), 0)
        for k in range(N_GROUPS):
            start = jnp.where(g81 == k, run, start)
            nblk_ref[0, k] = nblk[k, 0]
            nblk_ref[1, k] = run[0, 0]
            run = run + nblk[k:k + 1, :]
        pos = jnp.sum(jnp.where(onehot, rank + (start * blk).astype(F32), 0.0), axis=0, keepdims=True)
        pos_ref[...] = pos.astype(jnp.int32)

    first = nblk_ref[1, grp]

    def block(b, carry):
        rows = (first + b) * blk + lax.broadcasted_iota(jnp.int32, (blk, tm), 0)
        p = jnp.where(rows == pos_ref[...], 1.0, 0.0).astype(BF16)
        xs = _dot(p, h2_ref[...]).astype(BF16)
        cw2 = _dot(p, cw_ref[...])
        cw = cw2[:, :LANES] + cw2[:, LANES:]
        gate = _dot(xs, wg_ref[0])
        up = _dot(xs, wu_ref[0])
        lane = lax.broadcasted_iota(jnp.int32, cw.shape, 1)
        for e in range(EXPERTS_PER_GROUP):
            lo, hi = e * D_EXPERT, (e + 1) * D_EXPERT
            w_e = jnp.sum(jnp.where(lane == grp * EXPERTS_PER_GROUP + e, cw, 0.0), axis=-1, keepdims=True)
            hh_ref[:, lo:hi] = (jax.nn.silu(gate[:, lo:hi]) * up[:, lo:hi] * w_e).astype(BF16)
        yb = _dot(hh_ref[...], wd_ref[0]).astype(BF16)
        acc_ref[...] += _dot_tn(p, yb)
        return carry

    lax.fori_loop(0, nblk_ref[0, grp], block, 0)

    @pl.when(grp == N_GROUPS - 1)
    def _():
        y_ref[...] = _rms(x1_ref[...] + acc_ref[...], fn_ref[...])


def _moe(x1, h2, comb, wts, tm, blk):
    t = x1.shape[0]
    row = lambda w, **kw: pl.BlockSpec((tm, w), lambda i, g: (i, 0), **kw)
    return pl.pallas_call(
        functools.partial(_moe_kernel, blk=blk),
        name="moe",
        grid=(t // tm, N_GROUPS),
        in_specs=[row(D_MODEL, pipeline_mode=pl.Buffered(1)), row(D_MODEL), row(LANES),
                  pl.BlockSpec((1, D_MODEL, GROUP_FF), lambda i, g: (g, 0, 0)),
                  pl.BlockSpec((1, D_MODEL, GROUP_FF), lambda i, g: (g, 0, 0)),
                  pl.BlockSpec((1, GROUP_FF, D_MODEL), lambda i, g: (g, 0, 0)),
                  pl.BlockSpec((1, D_MODEL), lambda i, g: (0, 0))],
        out_specs=row(D_MODEL),
        out_shape=jax.ShapeDtypeStruct((t, D_MODEL), F32),
        scratch_shapes=[pltpu.VMEM((tm, D_MODEL), F32), pltpu.VMEM((1, tm), jnp.int32),
                        pltpu.VMEM((tm, 2 * LANES), BF16), pltpu.VMEM((tm, tm), BF16),
                        pltpu.VMEM((blk, GROUP_FF), BF16), pltpu.SMEM((2, N_GROUPS), jnp.int32)],
        compiler_params=_params(("arbitrary", "arbitrary")),
    )(x1, h2, comb, wts['w_gate_e'], wts['w_up_e'], wts['w_down_e'], wts['fn'])


def _rope_tables(seq_len):
    pos = jnp.arange(seq_len, dtype=F32)
    inv_freq = 1.0 / (ROPE_THETA ** (jnp.arange(0, QK_ROPE, 2, dtype=F32) / QK_ROPE))
    ang = pos[:, None] * inv_freq[None, :]
    cos, sin = jnp.cos(ang), jnp.sin(ang)
    pad = jnp.zeros((seq_len, HEAD_PAD - QK_DIM), F32)
    cos_t = jnp.concatenate([jnp.ones((seq_len, QK_NOPE), F32), cos, cos, pad], axis=-1)
    sin_t = jnp.concatenate([jnp.zeros((seq_len, QK_NOPE), F32), sin, sin, pad], axis=-1)
    return cos_t, sin_t


def _rot_half_cols(w):
    half = QK_ROPE // 2
    return jnp.concatenate([-w[..., half:], w[..., :half]], axis=-1)


def _prep_weights(attn_norm, w_in, q_norm, w_uq, kv_norm, w_ukv, lb_param, hg_norm, w_branch, w_out, ffn_norm,
                  w_group, b_group, w_expert, b_expert, w_gate, w_up, w_down, final_norm):
    w_in = w_in[0]
    o_kpe = Q_LORA + KV_LORA
    o_hg = o_kpe + QK_ROPE
    o_gate = o_hg + 3 * HG_KW + 2 * HG_WIDTH
    w_kpe = w_in[:, o_kpe:o_hg]
    zpad = lambda n: jnp.zeros((D_MODEL, n), F32)
    place = lambda w: jnp.concatenate([zpad(QK_NOPE), w, zpad(HEAD_PAD - QK_DIM)], axis=-1)
    w_kpe2 = jnp.concatenate([place(w_kpe), place(_rot_half_cols(w_kpe))], axis=-1)

    uq = w_uq[0]
    zq = lambda n: jnp.zeros((Q_LORA, MLA_HEADS, n), F32)
    uq_full = jnp.concatenate([uq, zq(HEAD_PAD - QK_DIM)], axis=-1)
    uq_rot = jnp.concatenate([zq(QK_NOPE), _rot_half_cols(uq[..., QK_NOPE:]), zq(HEAD_PAD - QK_DIM)], axis=-1)
    w_uq2 = jnp.concatenate([uq_full.reshape(Q_LORA, ATT_W), uq_rot.reshape(Q_LORA, ATT_W)], axis=-1)

    ukv = w_ukv[0]
    zk = lambda n: jnp.zeros((KV_LORA, MLA_HEADS, n), F32)
    uk = jnp.concatenate([ukv[..., :QK_NOPE], zk(HEAD_PAD - QK_NOPE)], axis=-1)
    uv = jnp.concatenate([ukv[..., QK_NOPE:], zk(HEAD_PAD - V_DIM)], axis=-1)
    w_ukv2 = jnp.concatenate([uk.reshape(KV_LORA, ATT_W), uv.reshape(KV_LORA, ATT_W)], axis=-1)

    wb0 = w_branch[0, 0].reshape(MLA_HEADS, V_DIM, D_MODEL)
    wb0 = jnp.concatenate([wb0, jnp.zeros((MLA_HEADS, HEAD_PAD - V_DIM, D_MODEL), F32)], axis=1)

    w_router = jnp.concatenate([w_expert[0], w_group[0],
                                jnp.zeros((D_MODEL, LANES - N_EXPERTS - N_GROUPS), F32)], axis=-1)
    b_router = jnp.concatenate([b_expert[0], b_group[0], jnp.zeros((LANES - N_EXPERTS - N_GROUPS,), F32)])
    w_router_hi = w_router.astype(BF16)
    w_router_lo = (w_router - w_router_hi.astype(F32)).astype(BF16)
    w_router = jnp.concatenate([w_router_hi, w_router_lo], axis=-1)

    ff = lambda w: jnp.transpose(w[0], (0, 2, 1, 3)).reshape(N_GROUPS, D_MODEL, GROUP_FF).astype(BF16)
    lbs = lb_param.astype(F32)
    return {
        'an': attn_norm[0][None], 'qn': q_norm[0][None], 'kvn': kv_norm[0][None],
        'w_lat': w_in[:, :o_kpe].astype(BF16), 'w_kpe': w_kpe2.astype(BF16),
        'w_hg': w_in[:, o_hg:o_gate].astype(BF16), 'w_gate': w_in[:, o_gate:].astype(BF16),
        'w_uq': w_uq2.astype(BF16), 'w_ukv': w_ukv2.astype(BF16), 'lbs': lbs,
        'hgn': hg_norm[0][None], 'ffn': ffn_norm[0][None],
        'wb0': wb0.reshape(ATT_W, D_MODEL).astype(BF16), 'wb1': w_branch[0, 1].astype(BF16),
        'w_out': w_out[0].astype(BF16), 'w_router': w_router, 'b_router': b_router[None],
        'w_gate_e': ff(w_gate), 'w_up_e': ff(w_up),
        'w_down_e': w_down[0].reshape(N_GROUPS, GROUP_FF, D_MODEL).astype(BF16), 'fn': final_norm[None],
    }


TOKEN_TILE = 512
ATT_TQ = 512
ATT_TK = 512
ATT_UNROLL = 4
GLA_ROWS = 256
GLA_CHUNK = 64
MOE_TILE = 1024
MOE_BLOCK = 128


def _tile(n, pref):
    t = min(n, pref)
    assert n % t == 0
    return t


def _encoder(x, wts):
    b, l, d = x.shape
    t = b * l
    x2d = x.reshape(t, d)
    tm = _tile(l, TOKEN_TILE)
    q, k, v, hq, hi, fzf, fzb, hg, ga, gb = _in_proj(x2d, l, _rope_tables(l), wts, tm)
    r3 = lambda a: a.reshape(b, l, a.shape[-1])
    oa = _attention(r3(q), r3(k), r3(v), _tile(l, ATT_TQ), _tile(l, ATT_TK))
    of, ob = _gla(r3(hq), r3(hi), r3(fzf), r3(fzb), wts['lbs'], _tile(l, GLA_ROWS), GLA_CHUNK)
    x1, h2, comb = _merge(x2d, oa.reshape(t, ATT_W), of.reshape(t, HG_WIDTH), ob.reshape(t, HG_WIDTH),
                          hg, ga, gb, wts, tm)
    y = _moe(x1, h2, comb, wts, _tile(l, MOE_TILE), MOE_BLOCK)
    return y.reshape(b, l, d)


def kernel(x_prompt, x_sample, attn_norm, w_in, q_norm, w_uq, kv_norm, w_ukv, lb_param, hg_norm, w_branch, w_out,
           ffn_norm, w_group, b_group, w_expert, b_expert, w_gate, w_up, w_down, final_norm):
    wts = _prep_weights(attn_norm, w_in, q_norm, w_uq, kv_norm, w_ukv, lb_param, hg_norm, w_branch, w_out,
                        ffn_norm, w_group, b_group, w_expert, b_expert, w_gate, w_up, w_down, final_norm)
    return (_encoder(x_prompt, wts), _encoder(x_sample, wts))
```

```python
import functools
import math

import jax
import jax.numpy as jnp
from jax import lax
from jax.experimental import pallas as pl
from jax.experimental.pallas import tpu as pltpu

D_MODEL = 1024
MLA_HEADS = 8
Q_LORA = 384
KV_LORA = 256
QK_NOPE = 64
QK_ROPE = 32
V_DIM = 64
QK_DIM = QK_NOPE + QK_ROPE
ROPE_THETA = 10000.0
HG_HEADS = 4
HG_DK = 128
HG_DV = 128
HG_KW = HG_HEADS * HG_DK
HG_WIDTH = HG_HEADS * HG_DV
N_GROUPS = 4
EXPERTS_PER_GROUP = 8
N_EXPERTS = N_GROUPS * EXPERTS_PER_GROUP
D_EXPERT = 256
GROUP_FF = EXPERTS_PER_GROUP * D_EXPERT
EPS = 1e-6

LANES = 128
HEAD_PAD = LANES
ATT_W = MLA_HEADS * HEAD_PAD
VMEM_LIMIT = 56 * 1024 * 1024

F32 = jnp.float32
BF16 = jnp.bfloat16


def _dot(a, b):
    return jnp.dot(a, b, preferred_element_type=F32)


def _dot_nt(a, b):
    return lax.dot_general(a, b, (((1,), (1,)), ((), ())), preferred_element_type=F32)


def _dot_tn(a, b):
    return lax.dot_general(a, b, (((0,), (0,)), ((), ())), preferred_element_type=F32)


def _rms(x, g):
    return x * lax.rsqrt(jnp.mean(x * x, axis=-1, keepdims=True) + EPS) * g


def _const_spec(shape):
    zeros = (0,) * len(shape)
    return pl.BlockSpec(shape, lambda *_: zeros, pipeline_mode=pl.Buffered(1))


def _params(sem):
    return pltpu.CompilerParams(dimension_semantics=sem, vmem_limit_bytes=VMEM_LIMIT)


def _in_proj_kernel(x_ref, cos_ref, sin_ref, an_ref, qn_ref, kvn_ref, w_lat_ref, w_kpe_ref, w_hg_ref,
                    w_gate_ref, w_uq_ref, w_ukv_ref,
                    q_ref, k_ref, v_ref, hq_ref, hi_ref, fzf_ref, fzb_ref, hg_ref, ga_ref, gb_ref):
    h = _rms(x_ref[...], an_ref[...]).astype(BF16)
    cos = cos_ref[...]
    sin = sin_ref[...]

    lat = _dot(h, w_lat_ref[...])
    cqn = _rms(lat[:, :Q_LORA], qn_ref[...]).astype(BF16)
    ckvn = _rms(lat[:, Q_LORA:], kvn_ref[...]).astype(BF16)

    kp = _dot(h, w_kpe_ref[...])
    k_rot = kp[:, :HEAD_PAD] * cos + kp[:, HEAD_PAD:] * sin
    q2 = _dot(cqn, w_uq_ref[...])
    kv2 = _dot(ckvn, w_ukv_ref[...])
    scale = QK_DIM ** -0.5 * math.log2(math.e)
    for hd in range(MLA_HEADS):
        lo, hi = hd * HEAD_PAD, (hd + 1) * HEAD_PAD
        q_ref[:, lo:hi] = ((q2[:, lo:hi] * cos + q2[:, ATT_W + lo:ATT_W + hi] * sin) * scale).astype(BF16)
        k_ref[:, lo:hi] = (kv2[:, lo:hi] + k_rot).astype(BF16)
    lane = lax.broadcasted_iota(jnp.int32, (1, ATT_W), 1)
    v_ref[...] = jnp.where(lane % HEAD_PAD == V_DIM, 1.0, kv2[:, ATT_W:]).astype(BF16)

    hz = _dot(h, w_hg_ref[...])
    hq_ref[...] = (hz[:, :HG_KW] * (HG_DK ** -0.5)).astype(BF16)
    fzf_ref[...] = hz[:, HG_KW:2 * HG_KW]
    fzb_ref[...] = hz[:, 2 * HG_KW:3 * HG_KW]
    hi_ref[...] = hz[:, 3 * HG_KW:3 * HG_KW + HG_WIDTH].astype(BF16)
    hg_ref[...] = hz[:, 3 * HG_KW + HG_WIDTH:].astype(BF16)

    gz = _dot(h, w_gate_ref[...])
    ga_ref[...] = gz[:, :D_MODEL].astype(BF16)
    gb_ref[...] = gz[:, D_MODEL:].astype(BF16)


def _in_proj(x2d, seq_len, tabs, wts, tm):
    t = x2d.shape[0]
    nl = seq_len // tm
    row = lambda w: pl.BlockSpec((tm, w), lambda i: (i, 0))
    tab = pl.BlockSpec((tm, HEAD_PAD), lambda i: (i % nl, 0))
    consts = [wts['an'], wts['qn'], wts['kvn'], wts['w_lat'], wts['w_kpe'], wts['w_hg'], wts['w_gate'],
              wts['w_uq'], wts['w_ukv']]
    out_w = [(ATT_W, BF16), (ATT_W, BF16), (ATT_W, BF16), (HG_KW, BF16), (HG_WIDTH, BF16), (HG_KW, F32),
             (HG_KW, F32), (HG_WIDTH, BF16), (D_MODEL, BF16), (D_MODEL, BF16)]
    return pl.pallas_call(
        _in_proj_kernel,
        name="in_proj",
        grid=(t // tm,),
        in_specs=[row(D_MODEL), tab, tab] + [_const_spec(c.shape) for c in consts],
        out_specs=[row(w) for w, _ in out_w],
        out_shape=[jax.ShapeDtypeStruct((t, w), d) for w, d in out_w],
        compiler_params=_params(("parallel",)),
    )(x2d, tabs[0], tabs[1], *consts)


def _attention_kernel(q_ref, k_ref, v_ref, o_ref, sa_ref, sb_ref, m_ref, acc_ref, *, tk, unroll):
    q = q_ref[0]
    nk = k_ref.shape[1] // tk

    def scores(j):
        off = pl.multiple_of(j * tk, tk)
        return _dot_nt(k_ref[0, pl.ds(off, tk), :], q)

    def absorb(s_ref, j):
        off = pl.multiple_of(j * tk, tk)
        s = s_ref[...]
        m_old = m_ref[...]
        m_new = jnp.maximum(m_old, jnp.max(s, axis=0, keepdims=True))
        p = jnp.exp2(s - m_new).astype(BF16)
        acc_ref[...] = jnp.exp2(m_old - m_new) * acc_ref[...] + _dot_tn(v_ref[0, pl.ds(off, tk), :], p)
        m_ref[...] = m_new

    m_ref[...] = jnp.full(m_ref.shape, -jnp.inf, F32)
    acc_ref[...] = jnp.zeros(acc_ref.shape, F32)
    sa_ref[...] = scores(0)
    bufs = (sa_ref, sb_ref)

    def block(base, last):
        for u in range(unroll):
            if not (last and u == unroll - 1):
                bufs[(u + 1) % 2][...] = scores(base + u + 1)
            absorb(bufs[u % 2], base + u)

    def body(i, carry):
        block(i * unroll, False)
        return carry

    lax.fori_loop(0, nk // unroll - 1, body, 0)
    block(nk - unroll, True)
    acc = acc_ref[...]
    o_ref[0] = (acc / acc[V_DIM:V_DIM + 1, :]).T.astype(o_ref.dtype)


def _attention(q, k, v, tq, tk):
    b, l, _ = q.shape
    nk = l // tk
    unroll = ATT_UNROLL if nk >= 3 * ATT_UNROLL else 2
    assert nk % unroll == 0
    qspec = pl.BlockSpec((1, tq, HEAD_PAD), lambda bi, h, i: (bi, i, h))
    kspec = pl.BlockSpec((1, l, HEAD_PAD), lambda bi, h, i: (bi, 0, h))
    return pl.pallas_call(
        functools.partial(_attention_kernel, tk=tk, unroll=unroll),
        name="attention",
        grid=(b, MLA_HEADS, l // tq),
        in_specs=[qspec, kspec, kspec],
        out_specs=qspec,
        out_shape=jax.ShapeDtypeStruct((b, l, ATT_W), BF16),
        scratch_shapes=[pltpu.VMEM((tk, tq), F32), pltpu.VMEM((tk, tq), F32), pltpu.VMEM((1, tq), F32),
                        pltpu.VMEM((HEAD_PAD, tq), F32)],
        compiler_params=_params(("parallel", "parallel", "arbitrary")),
    )(q, k, v)


def _gla_masks(c, reverse):
    t = lax.broadcasted_iota(jnp.int32, (c, HG_DK), 0)
    tau = (c - 1 - t) if reverse else t
    uppers = [((tau >> j) & 1) == 1 for j in range(int(math.log2(c)))]
    row = lax.broadcasted_iota(jnp.int32, (c, c), 0)
    col = lax.broadcasted_iota(jnp.int32, (c, c), 1)
    lev = 31 - lax.clz(row ^ col)
    lev = jnp.where((row <= col) if reverse else (row >= col), lev, -2)
    return uppers, lev


def _gla_chunk(q, v, z, lb, st_ref, masks, reverse):
    uppers, lev = masks
    c = q.shape[0]
    q = q.astype(F32)
    f = lb + (1.0 - lb) * jax.nn.sigmoid(z)
    kk = 1.0 - f

    def from_prev(a, m):
        return pltpu.roll(a, (c - m) if reverse else m, 0)

    def from_next(a, m):
        return pltpu.roll(a, m if reverse else (c - m), 0)

    kk16 = kk.astype(BF16)
    a_mat = jnp.where(lev == -1, _dot_nt(q.astype(BF16), kk16), 0.0)
    head = f
    tail = None
    tot = f
    for j, upper in enumerate(uppers):
        m = 1 << j
        kj = kk16 if tail is None else (kk * tail).astype(BF16)
        a_mat = jnp.where(lev == j, _dot_nt((q * head).astype(BF16), kj), a_mat)
        sib = jnp.where(upper, from_prev(tot, m), from_next(tot, m))
        head = jnp.where(upper, head * sib, head)
        tail = jnp.where(upper, 1.0, sib) if tail is None else jnp.where(upper, tail, tail * sib)
        tot = tot * sib

    st = st_ref[...]
    o = _dot(a_mat.astype(BF16), v) + _dot_nt((q * head).astype(BF16), st.astype(BF16))
    st_ref[...] = st * tot[0:1, :] + _dot_tn(v, (kk * tail).astype(BF16))
    return o


def _gla_kernel(qf_ref, vf_ref, zf_ref, qb_ref, vb_ref, zb_ref, lb_ref, of_ref, ob_ref, sf_ref, sb_ref, *, chunk):
    @pl.when(pl.program_id(2) == 0)
    def _():
        sf_ref[...] = jnp.zeros_like(sf_ref)
        sb_ref[...] = jnp.zeros_like(sb_ref)

    tb = qf_ref.shape[1]
    n = tb // chunk
    lbp = lb_ref[...]
    lbe = jnp.exp(lbp - jnp.max(lbp, axis=0, keepdims=True))
    lb = lbe[0] / jnp.sum(lbe, axis=0)
    lbf = lb[0:1, :]
    lbb = lb[1:2, :]
    masks_f = _gla_masks(chunk, False)
    masks_b = _gla_masks(chunk, True)
    for c in range(n):
        sl = pl.ds(c * chunk, chunk)
        of_ref[0, sl, :] = _gla_chunk(qf_ref[0, sl, :], vf_ref[0, sl, :], zf_ref[0, sl, :], lbf, sf_ref,
                                      masks_f, False)
        sl = pl.ds((n - 1 - c) * chunk, chunk)
        ob_ref[0, sl, :] = _gla_chunk(qb_ref[0, sl, :], vb_ref[0, sl, :], zb_ref[0, sl, :], lbb, sb_ref,
                                      masks_b, True)


def _gla(hq, hi, fzf, fzb, lbs, tb, chunk):
    b, l, _ = hq.shape
    nt = l // tb
    fwd = pl.BlockSpec((1, tb, HG_DK), lambda bi, h, i: (bi, i, h))
    bwd = pl.BlockSpec((1, tb, HG_DK), lambda bi, h, i: (bi, nt - 1 - i, h))
    lbspec = pl.BlockSpec((lbs.shape[0], 2, HG_DK), lambda bi, h, i: (0, 0, h))
    out = jax.ShapeDtypeStruct((b, l, HG_WIDTH), F32)
    return pl.pallas_call(
        functools.partial(_gla_kernel, chunk=chunk),
        name="gla",
        grid=(b, HG_HEADS, nt),
        in_specs=[fwd, fwd, fwd, bwd, bwd, bwd, lbspec],
        out_specs=[fwd, bwd],
        out_shape=[out, out],
        scratch_shapes=[pltpu.VMEM((HG_DV, HG_DK), F32), pltpu.VMEM((HG_DV, HG_DK), F32)],
        compiler_params=_params(("parallel", "parallel", "arbitrary")),
    )(hq, hi, fzf, hq, hi, fzb, lbs)


def _route(logits):
    lane = lax.broadcasted_iota(jnp.int32, logits.shape, 1)
    neg = -jnp.inf
    big = jnp.int32(LANES)

    is_g = (lane >= N_EXPERTS) & (lane < N_EXPERTS + N_GROUPS)
    gl = jnp.where(is_g, logits, neg)
    ge = jnp.exp(gl - jnp.max(gl, axis=-1, keepdims=True))
    gp = ge / jnp.sum(ge, axis=-1, keepdims=True)
    g_val = jnp.max(gp, axis=-1, keepdims=True)
    g_idx = jnp.min(jnp.where(is_g & (gp == g_val), lane, big), axis=-1, keepdims=True) - N_EXPERTS

    sel = (lane < N_EXPERTS) & ((lane >> 3) == g_idx)
    el = jnp.where(sel, logits, neg)
    ee = jnp.exp(el - jnp.max(el, axis=-1, keepdims=True))
    ep = ee / jnp.sum(ee, axis=-1, keepdims=True)
    v1 = jnp.max(ep, axis=-1, keepdims=True)
    i1 = jnp.min(jnp.where(sel & (ep == v1), lane, big), axis=-1, keepdims=True)
    rest = jnp.where(sel & (lane != i1), ep, -1.0)
    v2 = jnp.max(rest, axis=-1, keepdims=True)
    i2 = jnp.min(jnp.where(rest == v2, lane, big), axis=-1, keepdims=True)
    den = v1 + v2
    comb = jnp.where(lane == i1, g_val * v1 / den, 0.0) + jnp.where(lane == i2, g_val * v2 / den, 0.0)
    return comb + jnp.where(lane == N_EXPERTS, g_idx.astype(F32), 0.0)


def _merge_kernel(x_ref, oa_ref, of_ref, ob_ref, hg_ref, ga_ref, gb_ref, hgn_ref, ffn_ref, wb0_ref, wb1_ref,
                  wout_ref, wr_ref, br_ref, x1_ref, h2_ref, comb_ref):
    o = of_ref[...] + ob_ref[...]
    hgn = hgn_ref[...]
    parts = []
    for hd in range(HG_HEADS):
        lo, hi = hd * HG_DV, (hd + 1) * HG_DV
        parts.append(_rms(o[:, lo:hi], hgn[:, lo:hi]))
    o_b = (jnp.concatenate(parts, axis=-1) * jax.nn.silu(hg_ref[...].astype(F32))).astype(BF16)
    merged = (jax.nn.sigmoid(ga_ref[...].astype(F32)) * _dot(oa_ref[...], wb0_ref[...])
              + jax.nn.sigmoid(gb_ref[...].astype(F32)) * _dot(o_b, wb1_ref[...]))
    x1 = x_ref[...] + _dot(merged.astype(BF16), wout_ref[...])
    x1_ref[...] = x1
    h2 = _rms(x1, ffn_ref[...])
    h2_hi = h2.astype(BF16)
    h2_ref[...] = h2_hi
    h2_lo = (h2 - h2_hi.astype(F32)).astype(BF16)
    r = _dot(h2_hi, wr_ref[...])
    logits = r[:, :LANES] + r[:, LANES:] + _dot(h2_lo, wr_ref[:, :LANES]) + br_ref[...]
    comb_ref[...] = _route(logits)


def _merge(x2d, oa, of, ob, hg, ga, gb, wts, tm):
    t = x2d.shape[0]
    row = lambda w: pl.BlockSpec((tm, w), lambda i: (i, 0))
    consts = [wts['hgn'], wts['ffn'], wts['wb0'], wts['wb1'], wts['w_out'], wts['w_router'], wts['b_router']]
    return pl.pallas_call(
        _merge_kernel,
        name="merge",
        grid=(t // tm,),
        in_specs=[row(D_MODEL), row(ATT_W), row(HG_WIDTH), row(HG_WIDTH), row(HG_WIDTH), row(D_MODEL),
                  row(D_MODEL)] + [_const_spec(c.shape) for c in consts],
        out_specs=[row(D_MODEL), row(D_MODEL), row(LANES)],
        out_shape=[jax.ShapeDtypeStruct((t, D_MODEL), F32), jax.ShapeDtypeStruct((t, D_MODEL), BF16),
                   jax.ShapeDtypeStruct((t, LANES), F32)],
        compiler_params=_params(("parallel",)),
    )(x2d, oa, of, ob, hg, ga, gb, *consts)


def _moe_kernel(x1_ref, h2_ref, comb_ref, wg_ref, wu_ref, wd_ref, fn_ref, y_ref,
                acc_ref, pos_ref, cw_ref, tri_ref, hh_ref, nblk_ref, *, blk):
    tile = pl.program_id(0)
    grp = pl.program_id(1)
    tm = h2_ref.shape[0]

    @pl.when((tile == 0) & (grp == 0))
    def _():
        r = lax.broadcasted_iota(jnp.int32, (tm, tm), 0)
        c = lax.broadcasted_iota(jnp.int32, (tm, tm), 1)
        tri_ref[...] = jnp.where(r < c, 1.0, 0.0).astype(BF16)

    @pl.when(grp == 0)
    def _():
        acc_ref[...] = jnp.zeros_like(acc_ref)
        comb = comb_ref[...]
        hi = comb.astype(BF16)
        cw_ref[:, :LANES] = hi
        cw_ref[:, LANES:] = (comb - hi.astype(F32)).astype(BF16)
        gid = comb.T[N_EXPERTS:N_EXPERTS + 1, :].astype(jnp.int32)
        g8 = lax.broadcasted_iota(jnp.int32, (8, tm), 0)
        onehot = g8 == gid
        ones = jnp.where(onehot, 1.0, 0.0)
        rank = _dot(ones.astype(BF16), tri_ref[...])
        nblk = (jnp.sum(ones, axis=1, keepdims=True).astype(jnp.int32) + (blk - 1)) >> int(math.log2(blk))
        start = jnp.zeros((8, 1), jnp.int32)
        run = jnp.zeros((1, 1), jnp.int32)
        g81 = lax.broadcasted_iota(jnp.int32, (8, 1), 0)
        for k in range(N_GROUPS):
            start = jnp.where(g81 == k, run, start)
            nblk_ref[0, k] = nblk[k, 0]
            nblk_ref[1, k] = run[0, 0]
            run = run + nblk[k:k + 1, :]
        pos = jnp.sum(jnp.where(onehot, rank + (start * blk).astype(F32), 0.0), axis=0, keepdims=True)
        pos_ref[...] = pos.astype(jnp.int32)

    first = nblk_ref[1, grp]

    def block(b, carry):
        rows = (first + b) * blk + lax.broadcasted_iota(jnp.int32, (blk, tm), 0)
        p = jnp.where(rows == pos_ref[...], 1.0, 0.0).astype(BF16)
        xs = _dot(p, h2_ref[...]).astype(BF16)
        cw2 = _dot(p, cw_ref[...])
        cw = cw2[:, :LANES] + cw2[:, LANES:]
        gate = _dot(xs, wg_ref[0])
        up = _dot(xs, wu_ref[0])
        lane = lax.broadcasted_iota(jnp.int32, cw.shape, 1)
        for e in range(EXPERTS_PER_GROUP):
            lo, hi = e * D_EXPERT, (e + 1) * D_EXPERT
            w_e = jnp.sum(jnp.where(lane == grp * EXPERTS_PER_GROUP + e, cw, 0.0), axis=-1, keepdims=True)
            hh_ref[:, lo:hi] = (jax.nn.silu(gate[:, lo:hi]) * up[:, lo:hi] * w_e).astype(BF16)
        yb = _dot(hh_ref[...], wd_ref[0]).astype(BF16)
        acc_ref[...] += _dot_tn(p, yb)
        return carry

    lax.fori_loop(0, nblk_ref[0, grp], block, 0)

    @pl.when(grp == N_GROUPS - 1)
    def _():
        y_ref[...] = _rms(x1_ref[...] + acc_ref[...], fn_ref[...])


def _moe(x1, h2, comb, wts, tm, blk):
    t = x1.shape[0]
    row = lambda w, **kw: pl.BlockSpec((tm, w), lambda i, g: (i, 0), **kw)
    return pl.pallas_call(
        functools.partial(_moe_kernel, blk=blk),
        name="moe",
        grid=(t // tm, N_GROUPS),
        in_specs=[row(D_MODEL, pipeline_mode=pl.Buffered(1)), row(D_MODEL), row(LANES),
                  pl.BlockSpec((1, D_MODEL, GROUP_FF), lambda i, g: (g, 0, 0)),
                  pl.BlockSpec((1, D_MODEL, GROUP_FF), lambda i, g: (g, 0, 0)),
                  pl.BlockSpec((1, GROUP_FF, D_MODEL), lambda i, g: (g, 0, 0)),
                  pl.BlockSpec((1, D_MODEL), lambda i, g: (0, 0))],
        out_specs=row(D_MODEL),
        out_shape=jax.ShapeDtypeStruct((t, D_MODEL), F32),
        scratch_shapes=[pltpu.VMEM((tm, D_MODEL), F32), pltpu.VMEM((1, tm), jnp.int32),
                        pltpu.VMEM((tm, 2 * LANES), BF16), pltpu.VMEM((tm, tm), BF16),
                        pltpu.VMEM((blk, GROUP_FF), BF16), pltpu.SMEM((2, N_GROUPS), jnp.int32)],
        compiler_params=_params(("arbitrary", "arbitrary")),
    )(x1, h2, comb, wts['w_gate_e'], wts['w_up_e'], wts['w_down_e'], wts['fn'])


def _rope_tables(seq_len):
    pos = jnp.arange(seq_len, dtype=F32)
    inv_freq = 1.0 / (ROPE_THETA ** (jnp.arange(0, QK_ROPE, 2, dtype=F32) / QK_ROPE))
    ang = pos[:, None] * inv_freq[None, :]
    cos, sin = jnp.cos(ang), jnp.sin(ang)
    pad = jnp.zeros((seq_len, HEAD_PAD - QK_DIM), F32)
    cos_t = jnp.concatenate([jnp.ones((seq_len, QK_NOPE), F32), cos, cos, pad], axis=-1)
    sin_t = jnp.concatenate([jnp.zeros((seq_len, QK_NOPE), F32), sin, sin, pad], axis=-1)
    return cos_t, sin_t


def _rot_half_cols(w):
    half = QK_ROPE // 2
    return jnp.concatenate([-w[..., half:], w[..., :half]], axis=-1)


def _prep_weights(attn_norm, w_in, q_norm, w_uq, kv_norm, w_ukv, lb_param, hg_norm, w_branch, w_out, ffn_norm,
                  w_group, b_group, w_expert, b_expert, w_gate, w_up, w_down, final_norm):
    w_in = w_in[0]
    o_kpe = Q_LORA + KV_LORA
    o_hg = o_kpe + QK_ROPE
    o_gate = o_hg + 3 * HG_KW + 2 * HG_WIDTH
    w_kpe = w_in[:, o_kpe:o_hg]
    zpad = lambda n: jnp.zeros((D_MODEL, n), F32)
    place = lambda w: jnp.concatenate([zpad(QK_NOPE), w, zpad(HEAD_PAD - QK_DIM)], axis=-1)
    w_kpe2 = jnp.concatenate([place(w_kpe), place(_rot_half_cols(w_kpe))], axis=-1)

    uq = w_uq[0]
    zq = lambda n: jnp.zeros((Q_LORA, MLA_HEADS, n), F32)
    uq_full = jnp.concatenate([uq, zq(HEAD_PAD - QK_DIM)], axis=-1)
    uq_rot = jnp.concatenate([zq(QK_NOPE), _rot_half_cols(uq[..., QK_NOPE:]), zq(HEAD_PAD - QK_DIM)], axis=-1)
    w_uq2 = jnp.concatenate([uq_full.reshape(Q_LORA, ATT_W), uq_rot.reshape(Q_LORA, ATT_W)], axis=-1)

    ukv = w_ukv[0]
    zk = lambda n: jnp.zeros((KV_LORA, MLA_HEADS, n), F32)
    uk = jnp.concatenate([ukv[..., :QK_NOPE], zk(HEAD_PAD - QK_NOPE)], axis=-1)
    uv = jnp.concatenate([ukv[..., QK_NOPE:], zk(HEAD_PAD - V_DIM)], axis=-1)
    w_ukv2 = jnp.concatenate([uk.reshape(KV_LORA, ATT_W), uv.reshape(KV_LORA, ATT_W)], axis=-1)

    wb0 = w_branch[0, 0].reshape(MLA_HEADS, V_DIM, D_MODEL)
    wb0 = jnp.concatenate([wb0, jnp.zeros((MLA_HEADS, HEAD_PAD - V_DIM, D_MODEL), F32)], axis=1)

    w_router = jnp.concatenate([w_expert[0], w_group[0],
                                jnp.zeros((D_MODEL, LANES - N_EXPERTS - N_GROUPS), F32)], axis=-1)
    b_router = jnp.concatenate([b_expert[0], b_group[0], jnp.zeros((LANES - N_EXPERTS - N_GROUPS,), F32)])
    w_router_hi = w_router.astype(BF16)
    w_router_lo = (w_router - w_router_hi.astype(F32)).astype(BF16)
    w_router = jnp.concatenate([w_router_hi, w_router_lo], axis=-1)

    ff = lambda w: jnp.transpose(w[0], (0, 2, 1, 3)).reshape(N_GROUPS, D_MODEL, GROUP_FF).astype(BF16)
    lbs = lb_param.astype(F32)
    return {
        'an': attn_norm[0][None], 'qn': q_norm[0][None], 'kvn': kv_norm[0][None],
        'w_lat': w_in[:, :o_kpe].astype(BF16), 'w_kpe': w_kpe2.astype(BF16),
        'w_hg': w_in[:, o_hg:o_gate].astype(BF16), 'w_gate': w_in[:, o_gate:].astype(BF16),
        'w_uq': w_uq2.astype(BF16), 'w_ukv': w_ukv2.astype(BF16), 'lbs': lbs,
        'hgn': hg_norm[0][None], 'ffn': ffn_norm[0][None],
        'wb0': wb0.reshape(ATT_W, D_MODEL).astype(BF16), 'wb1': w_branch[0, 1].astype(BF16),
        'w_out': w_out[0].astype(BF16), 'w_router': w_router, 'b_router': b_router[None],
        'w_gate_e': ff(w_gate), 'w_up_e': ff(w_up),
        'w_down_e': w_down[0].reshape(N_GROUPS, GROUP_FF, D_MODEL).astype(BF16), 'fn': final_norm[None],
    }


TOKEN_TILE = 512
ATT_TQ = 1024
ATT_TK = 512
ATT_UNROLL = 4
GLA_ROWS = 512
GLA_CHUNK = 128
MOE_TILE = 1024
MOE_BLOCK = 128


def _tile(n, pref):
    t = min(n, pref)
    assert n % t == 0
    return t


def _encoder(x, wts):
    b, l, d = x.shape
    t = b * l
    x2d = x.reshape(t, d)
    tm = _tile(l, TOKEN_TILE)
    q, k, v, hq, hi, fzf, fzb, hg, ga, gb = _in_proj(x2d, l, _rope_tables(l), wts, tm)
    r3 = lambda a: a.reshape(b, l, a.shape[-1])
    oa = _attention(r3(q), r3(k), r3(v), _tile(l, ATT_TQ), _tile(l, ATT_TK))
    of, ob = _gla(r3(hq), r3(hi), r3(fzf), r3(fzb), wts['lbs'], _tile(l, GLA_ROWS), GLA_CHUNK)
    x1, h2, comb = _merge(x2d, oa.reshape(t, ATT_W), of.reshape(t, HG_WIDTH), ob.reshape(t, HG_WIDTH),
                          hg, ga, gb, wts, tm)
    y = _moe(x1, h2, comb, wts, _tile(l, MOE_TILE), MOE_BLOCK)
    return y.reshape(b, l, d)


def kernel(x_prompt, x_sample, attn_norm, w_in, q_norm, w_uq, kv_norm, w_ukv, lb_param, hg_norm, w_branch, w_out,
           ffn_norm, w_group, b_group, w_expert, b_expert, w_gate, w_up, w_down, final_norm):
    wts = _prep_weights(attn_norm, w_in, q_norm, w_uq, kv_norm, w_ukv, lb_param, hg_norm, w_branch, w_out,
                        ffn_norm, w_group, b_group, w_expert, b_expert, w_gate, w_up, w_down, final_norm)
    return (_encoder(x_prompt, wts), _encoder(x_sample, wts))
```

```python
import functools
import math

import jax
import jax.numpy as jnp
from jax import lax
from jax.experimental import pallas as pl
from jax.experimental.pallas import tpu as pltpu

D_MODEL = 1024
MLA_HEADS = 8
Q_LORA = 384
KV_LORA = 256
QK_NOPE = 64
QK_ROPE = 32
V_DIM = 64
QK_DIM = QK_NOPE + QK_ROPE
ROPE_THETA = 10000.0
HG_HEADS = 4
HG_DK = 128
HG_DV = 128
HG_KW = HG_HEADS * HG_DK
HG_WIDTH = HG_HEADS * HG_DV
N_GROUPS = 4
EXPERTS_PER_GROUP = 8
N_EXPERTS = N_GROUPS * EXPERTS_PER_GROUP
D_EXPERT = 256
GROUP_FF = EXPERTS_PER_GROUP * D_EXPERT
EPS = 1e-6

LANES = 128
SUBLANES = 8
SLAB_LEVEL = 3
HEAD_PAD = LANES
ATT_W = MLA_HEADS * HEAD_PAD
ATT_VROWS = V_DIM + 16
VMEM_LIMIT = 56 * 1024 * 1024

F32 = jnp.float32
BF16 = jnp.bfloat16


def _dot(a, b):
    return jnp.dot(a, b, preferred_element_type=F32)


def _dot_nt(a, b):
    return lax.dot_general(a, b, (((1,), (1,)), ((), ())), preferred_element_type=F32)


def _dot_tn(a, b):
    return lax.dot_general(a, b, (((0,), (0,)), ((), ())), preferred_element_type=F32)


def _rms(x, g):
    return x * lax.rsqrt(jnp.mean(x * x, axis=-1, keepdims=True) + EPS) * g


def _const_spec(shape):
    zeros = (0,) * len(shape)
    return pl.BlockSpec(shape, lambda *_: zeros, pipeline_mode=pl.Buffered(1))


def _params(sem):
    return pltpu.CompilerParams(dimension_semantics=sem, vmem_limit_bytes=VMEM_LIMIT)


def _in_proj_kernel(x_ref, cos_ref, sin_ref, an_ref, qn_ref, kvn_ref, w_lat_ref, w_kpe_ref, w_hg_ref,
                    w_gate_ref, w_uq_ref, w_uk_ref, w_uvt_ref,
                    q_ref, k_ref, vt_ref, hq_ref, hi_ref, fzf_ref, fzb_ref, hg_ref, ga_ref, gb_ref):
    h = _rms(x_ref[...], an_ref[...]).astype(BF16)
    cos = cos_ref[...]
    sin = sin_ref[...]

    lat = _dot(h, w_lat_ref[...])
    cqn = _rms(lat[:, :Q_LORA], qn_ref[...]).astype(BF16)
    ckvn = _rms(lat[:, Q_LORA:], kvn_ref[...]).astype(BF16)

    kp = _dot(h, w_kpe_ref[...])
    k_rot = kp[:, :HEAD_PAD] * cos + kp[:, HEAD_PAD:] * sin
    q2 = _dot(cqn, w_uq_ref[...])
    k_nope = _dot(ckvn, w_uk_ref[...])
    scale = QK_DIM ** -0.5 * math.log2(math.e)
    for hd in range(MLA_HEADS):
        lo, hi = hd * HEAD_PAD, (hd + 1) * HEAD_PAD
        q_ref[:, lo:hi] = ((q2[:, lo:hi] * cos + q2[:, ATT_W + lo:ATT_W + hi] * sin) * scale).astype(BF16)
        k_ref[:, lo:hi] = (k_nope[:, lo:hi] + k_rot).astype(BF16)
    v_t = _dot_nt(w_uvt_ref[...], ckvn)
    row = lax.broadcasted_iota(jnp.int32, (ATT_W, 1), 0)
    vt_ref[0, 0] = jnp.where(row % HEAD_PAD == V_DIM, 1.0, v_t).astype(BF16)

    hz = _dot(h, w_hg_ref[...])
    hq_ref[...] = (hz[:, :HG_KW] * (HG_DK ** -0.5)).astype(BF16)
    fzf_ref[...] = hz[:, HG_KW:2 * HG_KW]
    fzb_ref[...] = hz[:, 2 * HG_KW:3 * HG_KW]
    hi_ref[...] = hz[:, 3 * HG_KW:3 * HG_KW + HG_WIDTH].astype(BF16)
    hg_ref[...] = hz[:, 3 * HG_KW + HG_WIDTH:].astype(BF16)

    gz = _dot(h, w_gate_ref[...])
    ga_ref[...] = gz[:, :D_MODEL].astype(BF16)
    gb_ref[...] = gz[:, D_MODEL:].astype(BF16)


def _in_proj(x2d, seq_len, tabs, wts, tm):
    t = x2d.shape[0]
    nl = seq_len // tm
    row = lambda w: pl.BlockSpec((tm, w), lambda i: (i, 0))
    tab = pl.BlockSpec((tm, HEAD_PAD), lambda i: (i % nl, 0))
    consts = [wts['an'], wts['qn'], wts['kvn'], wts['w_lat'], wts['w_kpe'], wts['w_hg'], wts['w_gate'],
              wts['w_uq'], wts['w_uk'], wts['w_uvt']]
    out_w = [(ATT_W, BF16), (ATT_W, BF16), None, (HG_KW, BF16), (HG_WIDTH, BF16), (HG_KW, F32),
             (HG_KW, F32), (HG_WIDTH, BF16), (D_MODEL, BF16), (D_MODEL, BF16)]
    vt_spec = pl.BlockSpec((1, 1, ATT_W, tm), lambda i: (i // nl, i % nl, 0, 0))
    vt_shape = jax.ShapeDtypeStruct((t // seq_len, nl, ATT_W, tm), BF16)
    return pl.pallas_call(
        _in_proj_kernel,
        name="in_proj",
        grid=(t // tm,),
        in_specs=[row(D_MODEL), tab, tab] + [_const_spec(c.shape) for c in consts],
        out_specs=[vt_spec if o is None else row(o[0]) for o in out_w],
        out_shape=[vt_shape if o is None else jax.ShapeDtypeStruct((t, o[0]), o[1]) for o in out_w],
        compiler_params=_params(("parallel",)),
    )(x2d, tabs[0], tabs[1], *consts)


def _attention_kernel(q_ref, k_ref, vt_ref, o_ref, sa_ref, sb_ref, cmax_ref, m_ref, acc_ref, *, tk, unroll):
    q = q_ref[0]
    nk = k_ref.shape[1] // tk

    def scores(j):
        off = pl.multiple_of(j * tk, tk)
        return _dot_nt(k_ref[0, pl.ds(off, tk), :], q)

    bufs = (sa_ref, sb_ref)

    def fill(u, j):
        s = scores(j)
        bufs[u][...] = s
        cmax_ref[u:u + 1, :] = jnp.max(s, axis=0, keepdims=True)

    def absorb(u, j):
        m_old = m_ref[...]
        m_new = jnp.maximum(m_old, cmax_ref[u:u + 1, :])
        p = jnp.exp2(bufs[u][...] - m_new).astype(BF16)
        pv = _dot(vt_ref[0, j, :ATT_VROWS, :], p)
        acc_ref[:ATT_VROWS, :] = jnp.exp2(m_old - m_new) * acc_ref[:ATT_VROWS, :] + pv
        m_ref[...] = m_new

    m_ref[...] = jnp.full(m_ref.shape, -jnp.inf, F32)
    acc_ref[...] = jnp.zeros(acc_ref.shape, F32)
    fill(0, 0)

    def block(base, last):
        for u in range(unroll):
            if not (last and u == unroll - 1):
                fill((u + 1) % 2, base + u + 1)
            absorb(u % 2, base + u)

    def body(i, carry):
        block(i * unroll, False)
        return carry

    lax.fori_loop(0, nk // unroll - 1, body, 0)
    block(nk - unroll, True)
    acc = acc_ref[...]
    o_ref[0] = (acc / acc[V_DIM:V_DIM + 1, :]).T.astype(o_ref.dtype)


def _attention(q, k, vt, tq, tk):
    b, l, _ = q.shape
    nk = l // tk
    assert vt.shape == (b, nk, ATT_W, tk)
    vspec = pl.BlockSpec((1, nk, HEAD_PAD, tk), lambda bi, h, i: (bi, 0, h, 0))
    unroll = ATT_UNROLL if nk >= 3 * ATT_UNROLL else 2
    assert nk % unroll == 0
    qspec = pl.BlockSpec((1, tq, HEAD_PAD), lambda bi, h, i: (bi, i, h))
    kspec = pl.BlockSpec((1, l, HEAD_PAD), lambda bi, h, i: (bi, 0, h))
    return pl.pallas_call(
        functools.partial(_attention_kernel, tk=tk, unroll=unroll),
        name="attention",
        grid=(b, MLA_HEADS, l // tq),
        in_specs=[qspec, kspec, vspec],
        out_specs=qspec,
        out_shape=jax.ShapeDtypeStruct((b, l, ATT_W), BF16),
        scratch_shapes=[pltpu.VMEM((tk, tq), F32), pltpu.VMEM((tk, tq), F32), pltpu.VMEM((2, tq), F32),
                        pltpu.VMEM((1, tq), F32), pltpu.VMEM((HEAD_PAD, tq), F32)],
        compiler_params=_params(("parallel", "parallel", "arbitrary")),
    )(q, k, vt)


def _gla_masks(c, reverse):
    t = lax.broadcasted_iota(jnp.int32, (c, HG_DK), 0)
    tau = (c - 1 - t) if reverse else t
    uppers = [((tau >> j) & 1) == 1 for j in range(int(math.log2(c)))]
    row = lax.broadcasted_iota(jnp.int32, (c, c), 0)
    col = lax.broadcasted_iota(jnp.int32, (c, c), 1)
    lev = 31 - lax.clz(row ^ col)
    lev = jnp.where((row <= col) if reverse else (row >= col), lev, -2)
    lane = lax.broadcasted_iota(jnp.int32, (SUBLANES, c), 1)
    early = {}
    for j in range(SLAB_LEVEL, int(math.log2(c))):
        half = ((lane >> j) & 1) == (1 if reverse else 0)
        for b in range(c >> (j + 1)):
            early[j, b] = half & ((lane >> (j + 1)) == b)
    return uppers, lev, early


def _gla_chunk(q, v, z, lb, st_ref, masks, reverse):
    uppers, lev, early = masks
    c = q.shape[0]
    q = q.astype(F32)
    f = lb + (1.0 - lb) * jax.nn.sigmoid(z)
    kk = 1.0 - f

    def from_prev(a, m):
        return pltpu.roll(a, (c - m) if reverse else m, 0)

    def from_next(a, m):
        return pltpu.roll(a, m if reverse else (c - m), 0)

    kk16 = kk.astype(BF16)
    a_mat = jnp.where(lev == -1, _dot_nt(q.astype(BF16), kk16), 0.0)
    head = f
    tail = None
    tot = f
    for j, upper in enumerate(uppers[:SLAB_LEVEL]):
        m = 1 << j
        kj = kk16 if tail is None else (kk * tail).astype(BF16)
        a_mat = jnp.where(lev == j, _dot_nt((q * head).astype(BF16), kj), a_mat)
        sib = jnp.where(upper, from_prev(tot, m), from_next(tot, m))
        head = jnp.where(upper, head * sib, head)
        tail = jnp.where(upper, 1.0, sib) if tail is None else jnp.where(upper, tail, tail * sib)
        tot = tot * sib

    ns = c // SUBLANES
    slab = lambda a: [a[SUBLANES * i:SUBLANES * (i + 1)] for i in range(ns)]
    q_s, kk_s, head_s, tail_s, tot_s, a_s = slab(q), slab(kk), slab(head), slab(tail), slab(tot), slab(a_mat)
    zero16 = jnp.zeros((SUBLANES, HG_DK), BF16)
    for j in range(SLAB_LEVEL, len(uppers)):
        w = 1 << (j - SLAB_LEVEL)
        late, k_rows = [], []
        for b in range(ns // (2 * w)):
            first, second = range(2 * w * b, 2 * w * b + w), range(2 * w * b + w, 2 * w * (b + 1))
            lo, up = (second, first) if reverse else (first, second)
            late += [(i, b) for i in up]
            rows = {i: (kk_s[i] * tail_s[i]).astype(BF16) for i in lo}
            k_rows += [rows.get(i, zero16) for i in range(2 * w * b, 2 * w * (b + 1))]
        qc = jnp.concatenate([(q_s[i] * head_s[i]).astype(BF16) for i, _ in late], axis=0)
        pj = _dot_nt(qc, jnp.concatenate(k_rows, axis=0))
        for n, (i, b) in enumerate(late):
            a_s[i] = jnp.where(early[j, b], pj[SUBLANES * n:SUBLANES * (n + 1)], a_s[i])
        for b in range(ns // (2 * w)):
            first, second = range(2 * w * b, 2 * w * b + w), range(2 * w * b + w, 2 * w * (b + 1))
            lo, up = (second, first) if reverse else (first, second)
            t_lo, t_up = tot_s[lo[0]], tot_s[up[0]]
            for i in up:
                head_s[i] = head_s[i] * t_lo
            for i in lo:
                tail_s[i] = tail_s[i] * t_up
            both = t_lo * t_up
            for i in range(2 * w * b, 2 * w * (b + 1)):
                tot_s[i] = both
    a_mat = jnp.concatenate(a_s, axis=0)
    head = jnp.concatenate(head_s, axis=0)
    tail = jnp.concatenate(tail_s, axis=0)
    tot = tot_s[0]

    st = st_ref[...]
    o = _dot(a_mat.astype(BF16), v) + _dot_nt((q * head).astype(BF16), st.astype(BF16))
    st_ref[...] = st * tot[0:1, :] + _dot_tn(v, (kk * tail).astype(BF16))
    return o


def _gla_kernel(qf_ref, vf_ref, zf_ref, qb_ref, vb_ref, zb_ref, lb_ref, of_ref, ob_ref, sf_ref, sb_ref, *, chunk):
    @pl.when(pl.program_id(2) == 0)
    def _():
        sf_ref[...] = jnp.zeros_like(sf_ref)
        sb_ref[...] = jnp.zeros_like(sb_ref)

    tb = qf_ref.shape[1]
    n = tb // chunk
    lbp = lb_ref[...]
    lbe = jnp.exp(lbp - jnp.max(lbp, axis=0, keepdims=True))
    lb = lbe[0] / jnp.sum(lbe, axis=0)
    lbf = lb[0:1, :]
    lbb = lb[1:2, :]
    masks_f = _gla_masks(chunk, False)
    masks_b = _gla_masks(chunk, True)
    for c in range(n):
        sl = pl.ds(c * chunk, chunk)
        of_ref[0, sl, :] = _gla_chunk(qf_ref[0, sl, :], vf_ref[0, sl, :], zf_ref[0, sl, :], lbf, sf_ref,
                                      masks_f, False)
        sl = pl.ds((n - 1 - c) * chunk, chunk)
        ob_ref[0, sl, :] = _gla_chunk(qb_ref[0, sl, :], vb_ref[0, sl, :], zb_ref[0, sl, :], lbb, sb_ref,
                                      masks_b, True)


def _gla(hq, hi, fzf, fzb, lbs, tb, chunk):
    b, l, _ = hq.shape
    nt = l // tb
    fwd = pl.BlockSpec((1, tb, HG_DK), lambda bi, h, i: (bi, i, h))
    bwd = pl.BlockSpec((1, tb, HG_DK), lambda bi, h, i: (bi, nt - 1 - i, h))
    lbspec = pl.BlockSpec((lbs.shape[0], 2, HG_DK), lambda bi, h, i: (0, 0, h))
    out = jax.ShapeDtypeStruct((b, l, HG_WIDTH), F32)
    return pl.pallas_call(
        functools.partial(_gla_kernel, chunk=chunk),
        name="gla",
        grid=(b, HG_HEADS, nt),
        in_specs=[fwd, fwd, fwd, bwd, bwd, bwd, lbspec],
        out_specs=[fwd, bwd],
        out_shape=[out, out],
        scratch_shapes=[pltpu.VMEM((HG_DV, HG_DK), F32), pltpu.VMEM((HG_DV, HG_DK), F32)],
        compiler_params=_params(("parallel", "parallel", "arbitrary")),
    )(hq, hi, fzf, hq, hi, fzb, lbs)


def _route(logits):
    lane = lax.broadcasted_iota(jnp.int32, logits.shape, 1)
    neg = -jnp.inf
    big = jnp.int32(LANES)

    is_g = (lane >= N_EXPERTS) & (lane < N_EXPERTS + N_GROUPS)
    gl = jnp.where(is_g, logits, neg)
    ge = jnp.exp(gl - jnp.max(gl, axis=-1, keepdims=True))
    gp = ge / jnp.sum(ge, axis=-1, keepdims=True)
    g_val = jnp.max(gp, axis=-1, keepdims=True)
    g_idx = jnp.min(jnp.where(is_g & (gp == g_val), lane, big), axis=-1, keepdims=True) - N_EXPERTS

    sel = (lane < N_EXPERTS) & ((lane >> 3) == g_idx)
    el = jnp.where(sel, logits, neg)
    ee = jnp.exp(el - jnp.max(el, axis=-1, keepdims=True))
    ep = ee / jnp.sum(ee, axis=-1, keepdims=True)
    v1 = jnp.max(ep, axis=-1, keepdims=True)
    i1 = jnp.min(jnp.where(sel & (ep == v1), lane, big), axis=-1, keepdims=True)
    rest = jnp.where(sel & (lane != i1), ep, -1.0)
    v2 = jnp.max(rest, axis=-1, keepdims=True)
    i2 = jnp.min(jnp.where(rest == v2, lane, big), axis=-1, keepdims=True)
    den = v1 + v2
    comb = jnp.where(lane == i1, g_val * v1 / den, 0.0) + jnp.where(lane == i2, g_val * v2 / den, 0.0)
    return comb + jnp.where(lane == N_EXPERTS, g_idx.astype(F32), 0.0)


def _merge_kernel(x_ref, oa_ref, of_ref, ob_ref, hg_ref, ga_ref, gb_ref, hgn_ref, ffn_ref, wb0_ref, wb1_ref,
                  wout_ref, wr_ref, br_ref, x1_ref, h2_ref, comb_ref):
    o = of_ref[...] + ob_ref[...]
    hgn = hgn_ref[...]
    parts = []
    for hd in range(HG_HEADS):
        lo, hi = hd * HG_DV, (hd + 1) * HG_DV
        parts.append(_rms(o[:, lo:hi], hgn[:, lo:hi]))
    o_b = (jnp.concatenate(parts, axis=-1) * jax.nn.silu(hg_ref[...].astype(F32))).astype(BF16)
    merged = (jax.nn.sigmoid(ga_ref[...].astype(F32)) * _dot(oa_ref[...], wb0_ref[...])
              + jax.nn.sigmoid(gb_ref[...].astype(F32)) * _dot(o_b, wb1_ref[...]))
    x1 = x_ref[...] + _dot(merged.astype(BF16), wout_ref[...])
    x1_ref[...] = x1
    h2 = _rms(x1, ffn_ref[...])
    h2_hi = h2.astype(BF16)
    h2_ref[...] = h2_hi
    h2_lo = (h2 - h2_hi.astype(F32)).astype(BF16)
    r = _dot(h2_hi, wr_ref[...])
    logits = r[:, :LANES] + r[:, LANES:] + _dot(h2_lo, wr_ref[:, :LANES]) + br_ref[...]
    comb_ref[...] = _route(logits)


def _merge(x2d, oa, of, ob, hg, ga, gb, wts, tm):
    t = x2d.shape[0]
    row = lambda w: pl.BlockSpec((tm, w), lambda i: (i, 0))
    consts = [wts['hgn'], wts['ffn'], wts['wb0'], wts['wb1'], wts['w_out'], wts['w_router'], wts['b_router']]
    return pl.pallas_call(
        _merge_kernel,
        name="merge",
        grid=(t // tm,),
        in_specs=[row(D_MODEL), row(ATT_W), row(HG_WIDTH), row(HG_WIDTH), row(HG_WIDTH), row(D_MODEL),
                  row(D_MODEL)] + [_const_spec(c.shape) for c in consts],
        out_specs=[row(D_MODEL), row(D_MODEL), row(LANES)],
        out_shape=[jax.ShapeDtypeStruct((t, D_MODEL), F32), jax.ShapeDtypeStruct((t, D_MODEL), BF16),
                   jax.ShapeDtypeStruct((t, LANES), F32)],
        compiler_params=_params(("parallel",)),
    )(x2d, oa, of, ob, hg, ga, gb, *consts)


def _moe_kernel(x1_ref, h2_ref, comb_ref, wg_ref, wu_ref, wd_ref, fn_ref, y_ref,
                acc_ref, pos_ref, cw_ref, tri_ref, hh_ref, nblk_ref, *, blk):
    tile = pl.program_id(0)
    grp = pl.program_id(1)
    tm = h2_ref.shape[0]

    @pl.when((tile == 0) & (grp == 0))
    def _():
        r = lax.broadcasted_iota(jnp.int32, (tm, tm), 0)
        c = lax.broadcasted_iota(jnp.int32, (tm, tm), 1)
        tri_ref[...] = jnp.where(r < c, 1.0, 0.0).astype(BF16)

    @pl.when(grp == 0)
    def _():
        acc_ref[...] = jnp.zeros_like(acc_ref)
        comb = comb_ref[...]
        hi = comb.astype(BF16)
        cw_ref[:, :LANES] = hi
        cw_ref[:, LANES:] = (comb - hi.astype(F32)).astype(BF16)
        gid = comb.T[N_EXPERTS:N_EXPERTS + 1, :].astype(jnp.int32)
        g8 = lax.broadcasted_iota(jnp.int32, (8, tm), 0)
        onehot = g8 == gid
        ones = jnp.where(onehot, 1.0, 0.0)
        rank = _dot(ones.astype(BF16), tri_ref[...])
        nblk = (jnp.sum(ones, axis=1, keepdims=True).astype(jnp.int32) + (blk - 1)) >> int(math.log2(blk))
        start = jnp.zeros((8, 1), jnp.int32)
        run = jnp.zeros((1, 1), jnp.int32)
        g81 = lax.broadcasted_iota(jnp.int32, (8, 1), 0)
        for k in range(N_GROUPS):
            start = jnp.where(g81 == k, run, start)
            nblk_ref[0, k] = nblk[k, 0]
            nblk_ref[1, k] = run[0, 0]
            run = run + nblk[k:k + 1, :]
        pos = jnp.sum(jnp.where(onehot, rank + (start * blk).astype(F32), 0.0), axis=0, keepdims=True)
        pos_ref[...] = pos.astype(jnp.int32)

    first = nblk_ref[1, grp]

    def block(b, carry):
        rows = (first + b) * blk + lax.broadcasted_iota(jnp.int32, (blk, tm), 0)
        p = jnp.where(rows == pos_ref[...], 1.0, 0.0).astype(BF16)
        xs = _dot(p, h2_ref[...]).astype(BF16)
        cw2 = _dot(p, cw_ref[...])
        cw = cw2[:, :LANES] + cw2[:, LANES:]
        gate = _dot(xs, wg_ref[0])
        up = _dot(xs, wu_ref[0])
        lane = lax.broadcasted_iota(jnp.int32, cw.shape, 1)
        for e in range(EXPERTS_PER_GROUP):
            lo, hi = e * D_EXPERT, (e + 1) * D_EXPERT
            w_e = jnp.sum(jnp.where(lane == grp * EXPERTS_PER_GROUP + e, cw, 0.0), axis=-1, keepdims=True)
            hh_ref[:, lo:hi] = (jax.nn.silu(gate[:, lo:hi]) * up[:, lo:hi] * w_e).astype(BF16)
        yb = _dot(hh_ref[...], wd_ref[0]).astype(BF16)
        acc_ref[...] += _dot_tn(p, yb)
        return carry

    lax.fori_loop(0, nblk_ref[0, grp], block, 0)

    @pl.when(grp == N_GROUPS - 1)
    def _():
        y_ref[...] = _rms(x1_ref[...] + acc_ref[...], fn_ref[...])


def _moe(x1, h2, comb, wts, tm, blk):
    t = x1.shape[0]
    row = lambda w, **kw: pl.BlockSpec((tm, w), lambda i, g: (i, 0), **kw)
    return pl.pallas_call(
        functools.partial(_moe_kernel, blk=blk),
        name="moe",
        grid=(t // tm, N_GROUPS),
        in_specs=[row(D_MODEL, pipeline_mode=pl.Buffered(1)), row(D_MODEL), row(LANES),
                  pl.BlockSpec((1, D_MODEL, GROUP_FF), lambda i, g: (g, 0, 0)),
                  pl.BlockSpec((1, D_MODEL, GROUP_FF), lambda i, g: (g, 0, 0)),
                  pl.BlockSpec((1, GROUP_FF, D_MODEL), lambda i, g: (g, 0, 0)),
                  pl.BlockSpec((1, D_MODEL), lambda i, g: (0, 0))],
        out_specs=row(D_MODEL),
        out_shape=jax.ShapeDtypeStruct((t, D_MODEL), F32),
        scratch_shapes=[pltpu.VMEM((tm, D_MODEL), F32), pltpu.VMEM((1, tm), jnp.int32),
                        pltpu.VMEM((tm, 2 * LANES), BF16), pltpu.VMEM((tm, tm), BF16),
                        pltpu.VMEM((blk, GROUP_FF), BF16), pltpu.SMEM((2, N_GROUPS), jnp.int32)],
        compiler_params=_params(("arbitrary", "arbitrary")),
    )(x1, h2, comb, wts['w_gate_e'], wts['w_up_e'], wts['w_down_e'], wts['fn'])


def _rope_tables(seq_len):
    pos = jnp.arange(seq_len, dtype=F32)
    inv_freq = 1.0 / (ROPE_THETA ** (jnp.arange(0, QK_ROPE, 2, dtype=F32) / QK_ROPE))
    ang = pos[:, None] * inv_freq[None, :]
    cos, sin = jnp.cos(ang), jnp.sin(ang)
    pad = jnp.zeros((seq_len, HEAD_PAD - QK_DIM), F32)
    cos_t = jnp.concatenate([jnp.ones((seq_len, QK_NOPE), F32), cos, cos, pad], axis=-1)
    sin_t = jnp.concatenate([jnp.zeros((seq_len, QK_NOPE), F32), sin, sin, pad], axis=-1)
    return cos_t, sin_t


def _rot_half_cols(w):
    half = QK_ROPE // 2
    return jnp.concatenate([-w[..., half:], w[..., :half]], axis=-1)


def _prep_weights(attn_norm, w_in, q_norm, w_uq, kv_norm, w_ukv, lb_param, hg_norm, w_branch, w_out, ffn_norm,
                  w_group, b_group, w_expert, b_expert, w_gate, w_up, w_down, final_norm):
    w_in = w_in[0]
    o_kpe = Q_LORA + KV_LORA
    o_hg = o_kpe + QK_ROPE
    o_gate = o_hg + 3 * HG_KW + 2 * HG_WIDTH
    w_kpe = w_in[:, o_kpe:o_hg]
    zpad = lambda n: jnp.zeros((D_MODEL, n), F32)
    place = lambda w: jnp.concatenate([zpad(QK_NOPE), w, zpad(HEAD_PAD - QK_DIM)], axis=-1)
    w_kpe2 = jnp.concatenate([place(w_kpe), place(_rot_half_cols(w_kpe))], axis=-1)

    uq = w_uq[0]
    zq = lambda n: jnp.zeros((Q_LORA, MLA_HEADS, n), F32)
    uq_full = jnp.concatenate([uq, zq(HEAD_PAD - QK_DIM)], axis=-1)
    uq_rot = jnp.concatenate([zq(QK_NOPE), _rot_half_cols(uq[..., QK_NOPE:]), zq(HEAD_PAD - QK_DIM)], axis=-1)
    w_uq2 = jnp.concatenate([uq_full.reshape(Q_LORA, ATT_W), uq_rot.reshape(Q_LORA, ATT_W)], axis=-1)

    ukv = w_ukv[0]
    zk = lambda n: jnp.zeros((KV_LORA, MLA_HEADS, n), F32)
    uk = jnp.concatenate([ukv[..., :QK_NOPE], zk(HEAD_PAD - QK_NOPE)], axis=-1)
    uv = jnp.concatenate([ukv[..., QK_NOPE:], zk(HEAD_PAD - V_DIM)], axis=-1)
    w_uk = uk.reshape(KV_LORA, ATT_W)
    w_uvt = uv.reshape(KV_LORA, ATT_W).T

    wb0 = w_branch[0, 0].reshape(MLA_HEADS, V_DIM, D_MODEL)
    wb0 = jnp.concatenate([wb0, jnp.zeros((MLA_HEADS, HEAD_PAD - V_DIM, D_MODEL), F32)], axis=1)

    w_router = jnp.concatenate([w_expert[0], w_group[0],
                                jnp.zeros((D_MODEL, LANES - N_EXPERTS - N_GROUPS), F32)], axis=-1)
    b_router = jnp.concatenate([b_expert[0], b_group[0], jnp.zeros((LANES - N_EXPERTS - N_GROUPS,), F32)])
    w_router_hi = w_router.astype(BF16)
    w_router_lo = (w_router - w_router_hi.astype(F32)).astype(BF16)
    w_router = jnp.concatenate([w_router_hi, w_router_lo], axis=-1)

    ff = lambda w: jnp.transpose(w[0], (0, 2, 1, 3)).reshape(N_GROUPS, D_MODEL, GROUP_FF).astype(BF16)
    lbs = lb_param.astype(F32)
    return {
        'an': attn_norm[0][None], 'qn': q_norm[0][None], 'kvn': kv_norm[0][None],
        'w_lat': w_in[:, :o_kpe].astype(BF16), 'w_kpe': w_kpe2.astype(BF16),
        'w_hg': w_in[:, o_hg:o_gate].astype(BF16), 'w_gate': w_in[:, o_gate:].astype(BF16),
        'w_uq': w_uq2.astype(BF16), 'w_uk': w_uk.astype(BF16), 'w_uvt': w_uvt.astype(BF16), 'lbs': lbs,
        'hgn': hg_norm[0][None], 'ffn': ffn_norm[0][None],
        'wb0': wb0.reshape(ATT_W, D_MODEL).astype(BF16), 'wb1': w_branch[0, 1].astype(BF16),
        'w_out': w_out[0].astype(BF16), 'w_router': w_router, 'b_router': b_router[None],
        'w_gate_e': ff(w_gate), 'w_up_e': ff(w_up),
        'w_down_e': w_down[0].reshape(N_GROUPS, GROUP_FF, D_MODEL).astype(BF16), 'fn': final_norm[None],
    }


TOKEN_TILE = 512
ATT_TQ = 1024
ATT_TK = 512
ATT_UNROLL = 4
GLA_ROWS = 512
GLA_CHUNK = 128
MOE_TILE = 1024
MOE_BLOCK = 128


def _tile(n, pref):
    t = min(n, pref)
    assert n % t == 0
    return t


def _encoder(x, wts):
    b, l, d = x.shape
    t = b * l
    x2d = x.reshape(t, d)
    tm = _tile(l, TOKEN_TILE)
    assert tm == _tile(l, ATT_TK)
    q, k, vt, hq, hi, fzf, fzb, hg, ga, gb = _in_proj(x2d, l, _rope_tables(l), wts, tm)
    r3 = lambda a: a.reshape(b, l, a.shape[-1])
    oa = _attention(r3(q), r3(k), vt, _tile(l, ATT_TQ), tm)
    of, ob = _gla(r3(hq), r3(hi), r3(fzf), r3(fzb), wts['lbs'], _tile(l, GLA_ROWS), GLA_CHUNK)
    x1, h2, comb = _merge(x2d, oa.reshape(t, ATT_W), of.reshape(t, HG_WIDTH), ob.reshape(t, HG_WIDTH),
                          hg, ga, gb, wts, tm)
    y = _moe(x1, h2, comb, wts, _tile(l, MOE_TILE), MOE_BLOCK)
    return y.reshape(b, l, d)


def kernel(x_prompt, x_sample, attn_norm, w_in, q_norm, w_uq, kv_norm, w_ukv, lb_param, hg_norm, w_branch, w_out,
           ffn_norm, w_group, b_group, w_expert, b_expert, w_gate, w_up, w_down, final_norm):
    wts = _prep_weights(attn_norm, w_in, q_norm, w_uq, kv_norm, w_ukv, lb_param, hg_norm, w_branch, w_out,
                        ffn_norm, w_group, b_group, w_expert, b_expert, w_gate, w_up, w_down, final_norm)
    return (_encoder(x_prompt, wts), _encoder(x_sample, wts))
```

```python
import functools
import math

import jax
import jax.numpy as jnp
from jax import lax
from jax.experimental import pallas as pl
from jax.experimental.pallas import tpu as pltpu

D_MODEL = 1024
MLA_HEADS = 8
Q_LORA = 384
KV_LORA = 256
QK_NOPE = 64
QK_ROPE = 32
V_DIM = 64
QK_DIM = QK_NOPE + QK_ROPE
ROPE_THETA = 10000.0
HG_HEADS = 4
HG_DK = 128
HG_DV = 128
HG_KW = HG_HEADS * HG_DK
HG_WIDTH = HG_HEADS * HG_DV
N_GROUPS = 4
EXPERTS_PER_GROUP = 8
N_EXPERTS = N_GROUPS * EXPERTS_PER_GROUP
D_EXPERT = 256
GROUP_FF = EXPERTS_PER_GROUP * D_EXPERT
EPS = 1e-6

LANES = 128
SUBLANES = 8
SLAB_LEVEL = 3
HEAD_PAD = LANES
ATT_W = MLA_HEADS * HEAD_PAD
ATT_VROWS = V_DIM + 16
VMEM_LIMIT = 56 * 1024 * 1024

F32 = jnp.float32
BF16 = jnp.bfloat16


def _dot(a, b):
    return jnp.dot(a, b, preferred_element_type=F32)


def _dot_nt(a, b):
    return lax.dot_general(a, b, (((1,), (1,)), ((), ())), preferred_element_type=F32)


def _dot_tn(a, b):
    return lax.dot_general(a, b, (((0,), (0,)), ((), ())), preferred_element_type=F32)


def _rms(x, g):
    return x * lax.rsqrt(jnp.mean(x * x, axis=-1, keepdims=True) + EPS) * g


def _const_spec(shape):
    zeros = (0,) * len(shape)
    return pl.BlockSpec(shape, lambda *_: zeros, pipeline_mode=pl.Buffered(1))


def _params(sem):
    return pltpu.CompilerParams(dimension_semantics=sem, vmem_limit_bytes=VMEM_LIMIT)


def _in_proj_kernel(x_ref, cos_ref, sin_ref, an_ref, qn_ref, kvn_ref, w_lat_ref, w_kpe_ref, w_hg_ref,
                    w_gate_ref, w_uq_ref, w_uk_ref, w_uvt_ref,
                    q_ref, k_ref, vt_ref, hq_ref, hi_ref, fzf_ref, fzb_ref, hg_ref, ga_ref, gb_ref):
    h = _rms(x_ref[...], an_ref[...]).astype(BF16)
    cos = cos_ref[...]
    sin = sin_ref[...]

    lat = _dot(h, w_lat_ref[...])
    cqn = _rms(lat[:, :Q_LORA], qn_ref[...]).astype(BF16)
    ckvn = _rms(lat[:, Q_LORA:], kvn_ref[...]).astype(BF16)

    kp = _dot(h, w_kpe_ref[...])
    k_rot = kp[:, :HEAD_PAD] * cos + kp[:, HEAD_PAD:] * sin
    q2 = _dot(cqn, w_uq_ref[...])
    k_nope = _dot(ckvn, w_uk_ref[...])
    scale = QK_DIM ** -0.5 * math.log2(math.e)
    for hd in range(MLA_HEADS):
        lo, hi = hd * HEAD_PAD, (hd + 1) * HEAD_PAD
        q_ref[:, lo:hi] = ((q2[:, lo:hi] * cos + q2[:, ATT_W + lo:ATT_W + hi] * sin) * scale).astype(BF16)
        k_ref[:, lo:hi] = (k_nope[:, lo:hi] + k_rot).astype(BF16)
    v_t = _dot_nt(w_uvt_ref[...], ckvn)
    row = lax.broadcasted_iota(jnp.int32, (ATT_W, 1), 0)
    vt_ref[0, 0] = jnp.where(row % HEAD_PAD == V_DIM, 1.0, v_t).astype(BF16)

    hz = _dot(h, w_hg_ref[...])
    hq_ref[...] = (hz[:, :HG_KW] * (HG_DK ** -0.5)).astype(BF16)
    fzf_ref[...] = hz[:, HG_KW:2 * HG_KW]
    fzb_ref[...] = hz[:, 2 * HG_KW:3 * HG_KW]
    hi_ref[...] = hz[:, 3 * HG_KW:3 * HG_KW + HG_WIDTH].astype(BF16)
    hg_ref[...] = hz[:, 3 * HG_KW + HG_WIDTH:].astype(BF16)

    gz = _dot(h, w_gate_ref[...])
    ga_ref[...] = gz[:, :D_MODEL].astype(BF16)
    gb_ref[...] = gz[:, D_MODEL:].astype(BF16)


def _in_proj(x2d, seq_len, tabs, wts, tm):
    t = x2d.shape[0]
    nl = seq_len // tm
    row = lambda w: pl.BlockSpec((tm, w), lambda i: (i, 0))
    tab = pl.BlockSpec((tm, HEAD_PAD), lambda i: (i % nl, 0))
    consts = [wts['an'], wts['qn'], wts['kvn'], wts['w_lat'], wts['w_kpe'], wts['w_hg'], wts['w_gate'],
              wts['w_uq'], wts['w_uk'], wts['w_uvt']]
    out_w = [(ATT_W, BF16), (ATT_W, BF16), None, (HG_KW, BF16), (HG_WIDTH, BF16), (HG_KW, F32),
             (HG_KW, F32), (HG_WIDTH, BF16), (D_MODEL, BF16), (D_MODEL, BF16)]
    vt_spec = pl.BlockSpec((1, 1, ATT_W, tm), lambda i: (i // nl, i % nl, 0, 0))
    vt_shape = jax.ShapeDtypeStruct((t // seq_len, nl, ATT_W, tm), BF16)
    return pl.pallas_call(
        _in_proj_kernel,
        name="in_proj",
        grid=(t // tm,),
        in_specs=[row(D_MODEL), tab, tab] + [_const_spec(c.shape) for c in consts],
        out_specs=[vt_spec if o is None else row(o[0]) for o in out_w],
        out_shape=[vt_shape if o is None else jax.ShapeDtypeStruct((t, o[0]), o[1]) for o in out_w],
        compiler_params=_params(("parallel",)),
    )(x2d, tabs[0], tabs[1], *consts)


def _attention_kernel(q_ref, k_ref, vt_ref, o_ref, sa_ref, sb_ref, cmax_ref, m_ref, acc_ref, *, tk, unroll):
    q = q_ref[0]
    nk = k_ref.shape[1] // tk

    def scores(j):
        off = pl.multiple_of(j * tk, tk)
        return _dot_nt(k_ref[0, pl.ds(off, tk), :], q)

    bufs = (sa_ref, sb_ref)

    def fill(u, j):
        s = scores(j)
        bufs[u][...] = s
        cmax_ref[u:u + 1, :] = jnp.max(s, axis=0, keepdims=True)

    def absorb(u, j):
        m_old = m_ref[...]
        m_new = jnp.maximum(m_old, cmax_ref[u:u + 1, :])
        p = jnp.exp2(bufs[u][...] - m_new).astype(BF16)
        pv = _dot(vt_ref[0, j, :ATT_VROWS, :], p)
        acc_ref[:ATT_VROWS, :] = jnp.exp2(m_old - m_new) * acc_ref[:ATT_VROWS, :] + pv
        m_ref[...] = m_new

    m_ref[...] = jnp.full(m_ref.shape, -jnp.inf, F32)
    acc_ref[...] = jnp.zeros(acc_ref.shape, F32)
    fill(0, 0)

    def block(base, last):
        for u in range(unroll):
            if not (last and u == unroll - 1):
                fill((u + 1) % 2, base + u + 1)
            absorb(u % 2, base + u)

    def body(i, carry):
        block(i * unroll, False)
        return carry

    lax.fori_loop(0, nk // unroll - 1, body, 0)
    block(nk - unroll, True)
    acc = acc_ref[...]
    o_ref[0] = (acc / acc[V_DIM:V_DIM + 1, :]).T.astype(o_ref.dtype)


def _attention(q, k, vt, tq, tk):
    b, l, _ = q.shape
    nk = l // tk
    assert vt.shape == (b, nk, ATT_W, tk)
    vspec = pl.BlockSpec((1, nk, HEAD_PAD, tk), lambda bi, h, i: (bi, 0, h, 0))
    unroll = ATT_UNROLL if nk >= 3 * ATT_UNROLL else 2
    assert nk % unroll == 0
    qspec = pl.BlockSpec((1, tq, HEAD_PAD), lambda bi, h, i: (bi, i, h))
    kspec = pl.BlockSpec((1, l, HEAD_PAD), lambda bi, h, i: (bi, 0, h))
    return pl.pallas_call(
        functools.partial(_attention_kernel, tk=tk, unroll=unroll),
        name="attention",
        grid=(b, MLA_HEADS, l // tq),
        in_specs=[qspec, kspec, vspec],
        out_specs=qspec,
        out_shape=jax.ShapeDtypeStruct((b, l, ATT_W), BF16),
        scratch_shapes=[pltpu.VMEM((tk, tq), F32), pltpu.VMEM((tk, tq), F32), pltpu.VMEM((2, tq), F32),
                        pltpu.VMEM((1, tq), F32), pltpu.VMEM((HEAD_PAD, tq), F32)],
        compiler_params=_params(("parallel", "parallel", "arbitrary")),
    )(q, k, vt)


def _gla_masks(c, reverse):
    t = lax.broadcasted_iota(jnp.int32, (c, HG_DK), 0)
    tau = (c - 1 - t) if reverse else t
    uppers = [((tau >> j) & 1) == 1 for j in range(int(math.log2(c)))]
    row = lax.broadcasted_iota(jnp.int32, (c, c), 0)
    col = lax.broadcasted_iota(jnp.int32, (c, c), 1)
    lev = 31 - lax.clz(row ^ col)
    lev = jnp.where((row <= col) if reverse else (row >= col), lev, -2)
    lane = lax.broadcasted_iota(jnp.int32, (SUBLANES, c), 1)
    early = {}
    for j in range(SLAB_LEVEL, int(math.log2(c))):
        half = ((lane >> j) & 1) == (1 if reverse else 0)
        for b in range(c >> (j + 1)):
            early[j, b] = half & ((lane >> (j + 1)) == b)
    return uppers, lev, early


def _gla_chunk(q, v, z, lb, st_ref, masks, reverse):
    uppers, lev, early = masks
    c = q.shape[0]
    q = q.astype(F32)
    f = lb + (1.0 - lb) * jax.nn.sigmoid(z)
    kk = 1.0 - f

    def from_prev(a, m):
        return pltpu.roll(a, (c - m) if reverse else m, 0)

    def from_next(a, m):
        return pltpu.roll(a, m if reverse else (c - m), 0)

    kk16 = kk.astype(BF16)
    a_mat = jnp.where(lev == -1, _dot_nt(q.astype(BF16), kk16), 0.0)
    head = f
    tail = None
    tot = f
    for j, upper in enumerate(uppers[:SLAB_LEVEL]):
        m = 1 << j
        kj = kk16 if tail is None else (kk * tail).astype(BF16)
        a_mat = jnp.where(lev == j, _dot_nt((q * head).astype(BF16), kj), a_mat)
        sib = jnp.where(upper, from_prev(tot, m), from_next(tot, m))
        head = jnp.where(upper, head * sib, head)
        tail = jnp.where(upper, 1.0, sib) if tail is None else jnp.where(upper, tail, tail * sib)
        tot = tot * sib

    ns = c // SUBLANES
    slab = lambda a: [a[SUBLANES * i:SUBLANES * (i + 1)] for i in range(ns)]
    q_s, kk_s, head_s, tail_s, tot_s, a_s = slab(q), slab(kk), slab(head), slab(tail), slab(tot), slab(a_mat)
    zero16 = jnp.zeros((SUBLANES, HG_DK), BF16)
    for j in range(SLAB_LEVEL, len(uppers)):
        w = 1 << (j - SLAB_LEVEL)
        late, k_rows = [], []
        for b in range(ns // (2 * w)):
            first, second = range(2 * w * b, 2 * w * b + w), range(2 * w * b + w, 2 * w * (b + 1))
            lo, up = (second, first) if reverse else (first, second)
            late += [(i, b) for i in up]
            rows = {i: (kk_s[i] * tail_s[i]).astype(BF16) for i in lo}
            k_rows += [rows.get(i, zero16) for i in range(2 * w * b, 2 * w * (b + 1))]
        qc = jnp.concatenate([(q_s[i] * head_s[i]).astype(BF16) for i, _ in late], axis=0)
        pj = _dot_nt(qc, jnp.concatenate(k_rows, axis=0))
        for n, (i, b) in enumerate(late):
            a_s[i] = jnp.where(early[j, b], pj[SUBLANES * n:SUBLANES * (n + 1)], a_s[i])
        for b in range(ns // (2 * w)):
            first, second = range(2 * w * b, 2 * w * b + w), range(2 * w * b + w, 2 * w * (b + 1))
            lo, up = (second, first) if reverse else (first, second)
            t_lo, t_up = tot_s[lo[0]], tot_s[up[0]]
            for i in up:
                head_s[i] = head_s[i] * t_lo
            for i in lo:
                tail_s[i] = tail_s[i] * t_up
            both = t_lo * t_up
            for i in range(2 * w * b, 2 * w * (b + 1)):
                tot_s[i] = both
    a_mat = jnp.concatenate(a_s, axis=0)
    head = jnp.concatenate(head_s, axis=0)
    tail = jnp.concatenate(tail_s, axis=0)
    tot = tot_s[0]

    st = st_ref[...]
    o = _dot(a_mat.astype(BF16), v) + _dot_nt((q * head).astype(BF16), st.astype(BF16))
    st_ref[...] = st * tot[0:1, :] + _dot_tn(v, (kk * tail).astype(BF16))
    return o


def _gla_kernel(qf_ref, vf_ref, zf_ref, qb_ref, vb_ref, zb_ref, lb_ref, of_ref, ob_ref, sf_ref, sb_ref, *, chunk):
    @pl.when(pl.program_id(2) == 0)
    def _():
        sf_ref[...] = jnp.zeros_like(sf_ref)
        sb_ref[...] = jnp.zeros_like(sb_ref)

    tb = qf_ref.shape[1]
    n = tb // chunk
    lbp = lb_ref[...]
    lbe = jnp.exp(lbp - jnp.max(lbp, axis=0, keepdims=True))
    lb = lbe[0] / jnp.sum(lbe, axis=0)
    lbf = lb[0:1, :]
    lbb = lb[1:2, :]
    masks_f = _gla_masks(chunk, False)
    masks_b = _gla_masks(chunk, True)
    for c in range(n):
        sl = pl.ds(c * chunk, chunk)
        of_ref[0, sl, :] = _gla_chunk(qf_ref[0, sl, :], vf_ref[0, sl, :], zf_ref[0, sl, :], lbf, sf_ref,
                                      masks_f, False)
        sl = pl.ds((n - 1 - c) * chunk, chunk)
        ob_ref[0, sl, :] = _gla_chunk(qb_ref[0, sl, :], vb_ref[0, sl, :], zb_ref[0, sl, :], lbb, sb_ref,
                                      masks_b, True)


def _gla(hq, hi, fzf, fzb, lbs, tb, chunk):
    b, l, _ = hq.shape
    nt = l // tb
    fwd = pl.BlockSpec((1, tb, HG_DK), lambda bi, h, i: (bi, i, h))
    bwd = pl.BlockSpec((1, tb, HG_DK), lambda bi, h, i: (bi, nt - 1 - i, h))
    lbspec = pl.BlockSpec((lbs.shape[0], 2, HG_DK), lambda bi, h, i: (0, 0, h))
    out = jax.ShapeDtypeStruct((b, l, HG_WIDTH), F32)
    return pl.pallas_call(
        functools.partial(_gla_kernel, chunk=chunk),
        name="gla",
        grid=(b, HG_HEADS, nt),
        in_specs=[fwd, fwd, fwd, bwd, bwd, bwd, lbspec],
        out_specs=[fwd, bwd],
        out_shape=[out, out],
        scratch_shapes=[pltpu.VMEM((HG_DV, HG_DK), F32), pltpu.VMEM((HG_DV, HG_DK), F32)],
        compiler_params=_params(("parallel", "parallel", "arbitrary")),
    )(hq, hi, fzf, hq, hi, fzb, lbs)


def _route(logits):
    lane = lax.broadcasted_iota(jnp.int32, logits.shape, 1)
    neg = -jnp.inf
    big = jnp.int32(LANES)

    is_g = (lane >= N_EXPERTS) & (lane < N_EXPERTS + N_GROUPS)
    gl = jnp.where(is_g, logits, neg)
    ge = jnp.exp(gl - jnp.max(gl, axis=-1, keepdims=True))
    gp = ge / jnp.sum(ge, axis=-1, keepdims=True)
    g_val = jnp.max(gp, axis=-1, keepdims=True)
    g_idx = jnp.min(jnp.where(is_g & (gp == g_val), lane, big), axis=-1, keepdims=True) - N_EXPERTS

    sel = (lane < N_EXPERTS) & ((lane >> 3) == g_idx)
    el = jnp.where(sel, logits, neg)
    ee = jnp.exp(el - jnp.max(el, axis=-1, keepdims=True))
    ep = ee / jnp.sum(ee, axis=-1, keepdims=True)
    v1 = jnp.max(ep, axis=-1, keepdims=True)
    i1 = jnp.min(jnp.where(sel & (ep == v1), lane, big), axis=-1, keepdims=True)
    rest = jnp.where(sel & (lane != i1), ep, -1.0)
    v2 = jnp.max(rest, axis=-1, keepdims=True)
    i2 = jnp.min(jnp.where(rest == v2, lane, big), axis=-1, keepdims=True)
    den = v1 + v2
    comb = jnp.where(lane == i1, g_val * v1 / den, 0.0) + jnp.where(lane == i2, g_val * v2 / den, 0.0)
    return comb + jnp.where(lane == N_EXPERTS, g_idx.astype(F32), 0.0)


def _merge_kernel(x_ref, oa_ref, of_ref, ob_ref, hg_ref, ga_ref, gb_ref, hgn_ref, ffn_ref, wb0_ref, wb1_ref,
                  wout_ref, wr_ref, br_ref, x1_ref, h2_ref, comb_ref):
    o = of_ref[...] + ob_ref[...]
    hgn = hgn_ref[...]
    parts = []
    for hd in range(HG_HEADS):
        lo, hi = hd * HG_DV, (hd + 1) * HG_DV
        parts.append(_rms(o[:, lo:hi], hgn[:, lo:hi]))
    o_b = (jnp.concatenate(parts, axis=-1) * jax.nn.silu(hg_ref[...].astype(F32))).astype(BF16)
    merged = (jax.nn.sigmoid(ga_ref[...].astype(F32)) * _dot(oa_ref[...], wb0_ref[...])
              + jax.nn.sigmoid(gb_ref[...].astype(F32)) * _dot(o_b, wb1_ref[...]))
    x1 = x_ref[...] + _dot(merged.astype(BF16), wout_ref[...])
    x1_ref[...] = x1
    h2 = _rms(x1, ffn_ref[...])
    h2_hi = h2.astype(BF16)
    h2_ref[...] = h2_hi
    h2_lo = (h2 - h2_hi.astype(F32)).astype(BF16)
    r = _dot(h2_hi, wr_ref[...])
    logits = r[:, :LANES] + r[:, LANES:] + _dot(h2_lo, wr_ref[:, :LANES]) + br_ref[...]
    comb_ref[...] = _route(logits)


def _merge(x2d, oa, of, ob, hg, ga, gb, wts, tm):
    t = x2d.shape[0]
    row = lambda w: pl.BlockSpec((tm, w), lambda i: (i, 0))
    consts = [wts['hgn'], wts['ffn'], wts['wb0'], wts['wb1'], wts['w_out'], wts['w_router'], wts['b_router']]
    return pl.pallas_call(
        _merge_kernel,
        name="merge",
        grid=(t // tm,),
        in_specs=[row(D_MODEL), row(ATT_W), row(HG_WIDTH), row(HG_WIDTH), row(HG_WIDTH), row(D_MODEL),
                  row(D_MODEL)] + [_const_spec(c.shape) for c in consts],
        out_specs=[row(D_MODEL), row(D_MODEL), row(LANES)],
        out_shape=[jax.ShapeDtypeStruct((t, D_MODEL), F32), jax.ShapeDtypeStruct((t, D_MODEL), BF16),
                   jax.ShapeDtypeStruct((t, LANES), F32)],
        compiler_params=_params(("parallel",)),
    )(x2d, oa, of, ob, hg, ga, gb, *consts)


def _moe_kernel(x1_ref, h2_ref, comb_ref, wg_ref, wu_ref, wd_ref, fn_ref, y_ref,
                pos_ref, cw_ref, tri_ref, xs_ref, cws_ref, ys_ref, hh_ref, nblk_ref, *, blk):
    tile = pl.program_id(0)
    grp = pl.program_id(1)
    tm = h2_ref.shape[0]
    span = 2 * blk

    def one_hot(i):
        rows = i * span + lax.broadcasted_iota(jnp.int32, (span, tm), 0)
        return jnp.where(rows == pos_ref[...], 1.0, 0.0).astype(BF16)

    @pl.when((tile == 0) & (grp == 0))
    def _():
        r = lax.broadcasted_iota(jnp.int32, (tm, tm), 0)
        c = lax.broadcasted_iota(jnp.int32, (tm, tm), 1)
        tri_ref[...] = jnp.where(r < c, 1.0, 0.0).astype(BF16)

    @pl.when(grp == 0)
    def _():
        comb = comb_ref[...]
        hi = comb.astype(BF16)
        cw_ref[:, :LANES] = hi
        cw_ref[:, LANES:] = (comb - hi.astype(F32)).astype(BF16)
        gid = comb.T[N_EXPERTS:N_EXPERTS + 1, :].astype(jnp.int32)
        g8 = lax.broadcasted_iota(jnp.int32, (8, tm), 0)
        onehot = g8 == gid
        ones = jnp.where(onehot, 1.0, 0.0)
        rank = _dot(ones.astype(BF16), tri_ref[...])
        nblk = (jnp.sum(ones, axis=1, keepdims=True).astype(jnp.int32) + (blk - 1)) >> int(math.log2(blk))
        start = jnp.zeros((8, 1), jnp.int32)
        run = jnp.zeros((1, 1), jnp.int32)
        g81 = lax.broadcasted_iota(jnp.int32, (8, 1), 0)
        for k in range(N_GROUPS):
            start = jnp.where(g81 == k, run, start)
            nblk_ref[0, k] = nblk[k, 0]
            nblk_ref[1, k] = run[0, 0]
            run = run + nblk[k:k + 1, :]
        nblk_ref[0, N_GROUPS] = run[0, 0]
        pos = jnp.sum(jnp.where(onehot, rank + (start * blk).astype(F32), 0.0), axis=0, keepdims=True)
        pos_ref[...] = pos.astype(jnp.int32)

        def gather(i, carry):
            p = one_hot(i)
            at = pl.ds(pl.multiple_of(i * span, span), span)
            xs_ref[at, :] = _dot(p, h2_ref[...]).astype(BF16)
            cw2 = _dot(p, cw_ref[...])
            cws_ref[at, :] = cw2[:, :LANES] + cw2[:, LANES:]
            return carry

        lax.fori_loop(0, (run[0, 0] + 1) >> 1, gather, 0)

    first = nblk_ref[1, grp]

    def block(b, carry):
        at = pl.ds(pl.multiple_of((first + b) * blk, blk), blk)
        xs = xs_ref[at, :]
        cw = cws_ref[at, :]
        gate = _dot(xs, wg_ref[0])
        up = _dot(xs, wu_ref[0])
        lane = lax.broadcasted_iota(jnp.int32, cw.shape, 1)
        for e in range(EXPERTS_PER_GROUP):
            lo, hi = e * D_EXPERT, (e + 1) * D_EXPERT
            w_e = jnp.sum(jnp.where(lane == grp * EXPERTS_PER_GROUP + e, cw, 0.0), axis=-1, keepdims=True)
            hh_ref[:, lo:hi] = (jax.nn.silu(gate[:, lo:hi]) * up[:, lo:hi] * w_e).astype(BF16)
        ys_ref[at, :] = _dot(hh_ref[...], wd_ref[0]).astype(BF16)
        return carry

    lax.fori_loop(0, nblk_ref[0, grp], block, 0)

    @pl.when(grp == N_GROUPS - 1)
    def _():
        used = nblk_ref[0, N_GROUPS]
        ys_ref[pl.ds(pl.multiple_of(used * blk, blk), blk), :] = jnp.zeros((blk, D_MODEL), BF16)
        y_ref[...] = x1_ref[...]

        def scatter(i, carry):
            at = pl.ds(pl.multiple_of(i * span, span), span)
            y_ref[...] += _dot_tn(one_hot(i), ys_ref[at, :])
            return carry

        lax.fori_loop(0, (used + 1) >> 1, scatter, 0)
        y_ref[...] = _rms(y_ref[...], fn_ref[...])


def _moe(x1, h2, comb, wts, tm, blk):
    t = x1.shape[0]
    rows = tm + (N_GROUPS + 1) * blk
    rows += -rows % (2 * blk)
    once = pl.Buffered(1)
    row = lambda w, **kw: pl.BlockSpec((tm, w), lambda i, g: (i, 0), **kw)
    return pl.pallas_call(
        functools.partial(_moe_kernel, blk=blk),
        name="moe",
        grid=(t // tm, N_GROUPS),
        in_specs=[row(D_MODEL, pipeline_mode=once), row(D_MODEL, pipeline_mode=once), row(LANES),
                  pl.BlockSpec((1, D_MODEL, GROUP_FF), lambda i, g: (g, 0, 0)),
                  pl.BlockSpec((1, D_MODEL, GROUP_FF), lambda i, g: (g, 0, 0)),
                  pl.BlockSpec((1, GROUP_FF, D_MODEL), lambda i, g: (g, 0, 0)),
                  pl.BlockSpec((1, D_MODEL), lambda i, g: (0, 0))],
        out_specs=row(D_MODEL),
        out_shape=jax.ShapeDtypeStruct((t, D_MODEL), F32),
        scratch_shapes=[pltpu.VMEM((1, tm), jnp.int32), pltpu.VMEM((tm, 2 * LANES), BF16),
                        pltpu.VMEM((tm, tm), BF16), pltpu.VMEM((rows, D_MODEL), BF16),
                        pltpu.VMEM((rows, LANES), F32), pltpu.VMEM((rows, D_MODEL), BF16),
                        pltpu.VMEM((blk, GROUP_FF), BF16), pltpu.SMEM((2, N_GROUPS + 1), jnp.int32)],
        compiler_params=_params(("arbitrary", "arbitrary")),
    )(x1, h2, comb, wts['w_gate_e'], wts['w_up_e'], wts['w_down_e'], wts['fn'])


def _rope_tables(seq_len):
    pos = jnp.arange(seq_len, dtype=F32)
    inv_freq = 1.0 / (ROPE_THETA ** (jnp.arange(0, QK_ROPE, 2, dtype=F32) / QK_ROPE))
    ang = pos[:, None] * inv_freq[None, :]
    cos, sin = jnp.cos(ang), jnp.sin(ang)
    pad = jnp.zeros((seq_len, HEAD_PAD - QK_DIM), F32)
    cos_t = jnp.concatenate([jnp.ones((seq_len, QK_NOPE), F32), cos, cos, pad], axis=-1)
    sin_t = jnp.concatenate([jnp.zeros((seq_len, QK_NOPE), F32), sin, sin, pad], axis=-1)
    return cos_t, sin_t


def _rot_half_cols(w):
    half = QK_ROPE // 2
    return jnp.concatenate([-w[..., half:], w[..., :half]], axis=-1)


def _prep_weights(attn_norm, w_in, q_norm, w_uq, kv_norm, w_ukv, lb_param, hg_norm, w_branch, w_out, ffn_norm,
                  w_group, b_group, w_expert, b_expert, w_gate, w_up, w_down, final_norm):
    w_in = w_in[0]
    o_kpe = Q_LORA + KV_LORA
    o_hg = o_kpe + QK_ROPE
    o_gate = o_hg + 3 * HG_KW + 2 * HG_WIDTH
    w_kpe = w_in[:, o_kpe:o_hg]
    zpad = lambda n: jnp.zeros((D_MODEL, n), F32)
    place = lambda w: jnp.concatenate([zpad(QK_NOPE), w, zpad(HEAD_PAD - QK_DIM)], axis=-1)
    w_kpe2 = jnp.concatenate([place(w_kpe), place(_rot_half_cols(w_kpe))], axis=-1)

    uq = w_uq[0]
    zq = lambda n: jnp.zeros((Q_LORA, MLA_HEADS, n), F32)
    uq_full = jnp.concatenate([uq, zq(HEAD_PAD - QK_DIM)], axis=-1)
    uq_rot = jnp.concatenate([zq(QK_NOPE), _rot_half_cols(uq[..., QK_NOPE:]), zq(HEAD_PAD - QK_DIM)], axis=-1)
    w_uq2 = jnp.concatenate([uq_full.reshape(Q_LORA, ATT_W), uq_rot.reshape(Q_LORA, ATT_W)], axis=-1)

    ukv = w_ukv[0]
    zk = lambda n: jnp.zeros((KV_LORA, MLA_HEADS, n), F32)
    uk = jnp.concatenate([ukv[..., :QK_NOPE], zk(HEAD_PAD - QK_NOPE)], axis=-1)
    uv = jnp.concatenate([ukv[..., QK_NOPE:], zk(HEAD_PAD - V_DIM)], axis=-1)
    w_uk = uk.reshape(KV_LORA, ATT_W)
    w_uvt = uv.reshape(KV_LORA, ATT_W).T

    wb0 = w_branch[0, 0].reshape(MLA_HEADS, V_DIM, D_MODEL)
    wb0 = jnp.concatenate([wb0, jnp.zeros((MLA_HEADS, HEAD_PAD - V_DIM, D_MODEL), F32)], axis=1)

    w_router = jnp.concatenate([w_expert[0], w_group[0],
                                jnp.zeros((D_MODEL, LANES - N_EXPERTS - N_GROUPS), F32)], axis=-1)
    b_router = jnp.concatenate([b_expert[0], b_group[0], jnp.zeros((LANES - N_EXPERTS - N_GROUPS,), F32)])
    w_router_hi = w_router.astype(BF16)
    w_router_lo = (w_router - w_router_hi.astype(F32)).astype(BF16)
    w_router = jnp.concatenate([w_router_hi, w_router_lo], axis=-1)

    ff = lambda w: jnp.transpose(w[0], (0, 2, 1, 3)).reshape(N_GROUPS, D_MODEL, GROUP_FF).astype(BF16)
    lbs = lb_param.astype(F32)
    return {
        'an': attn_norm[0][None], 'qn': q_norm[0][None], 'kvn': kv_norm[0][None],
        'w_lat': w_in[:, :o_kpe].astype(BF16), 'w_kpe': w_kpe2.astype(BF16),
        'w_hg': w_in[:, o_hg:o_gate].astype(BF16), 'w_gate': w_in[:, o_gate:].astype(BF16),
        'w_uq': w_uq2.astype(BF16), 'w_uk': w_uk.astype(BF16), 'w_uvt': w_uvt.astype(BF16), 'lbs': lbs,
        'hgn': hg_norm[0][None], 'ffn': ffn_norm[0][None],
        'wb0': wb0.reshape(ATT_W, D_MODEL).astype(BF16), 'wb1': w_branch[0, 1].astype(BF16),
        'w_out': w_out[0].astype(BF16), 'w_router': w_router, 'b_router': b_router[None],
        'w_gate_e': ff(w_gate), 'w_up_e': ff(w_up),
        'w_down_e': w_down[0].reshape(N_GROUPS, GROUP_FF, D_MODEL).astype(BF16), 'fn': final_norm[None],
    }


TOKEN_TILE = 512
ATT_TQ = 1024
ATT_TK = 512
ATT_UNROLL = 4
GLA_ROWS = 1024
GLA_CHUNK = 128
MOE_TILE = 1024
MOE_BLOCK = 128


def _tile(n, pref):
    t = min(n, pref)
    assert n % t == 0
    return t


def _encoder(x, wts):
    b, l, d = x.shape
    t = b * l
    x2d = x.reshape(t, d)
    tm = _tile(l, TOKEN_TILE)
    assert tm == _tile(l, ATT_TK)
    q, k, vt, hq, hi, fzf, fzb, hg, ga, gb = _in_proj(x2d, l, _rope_tables(l), wts, tm)
    r3 = lambda a: a.reshape(b, l, a.shape[-1])
    oa = _attention(r3(q), r3(k), vt, _tile(l, ATT_TQ), tm)
    of, ob = _gla(r3(hq), r3(hi), r3(fzf), r3(fzb), wts['lbs'], _tile(l, GLA_ROWS), GLA_CHUNK)
    x1, h2, comb = _merge(x2d, oa.reshape(t, ATT_W), of.reshape(t, HG_WIDTH), ob.reshape(t, HG_WIDTH),
                          hg, ga, gb, wts, tm)
    y = _moe(x1, h2, comb, wts, _tile(l, MOE_TILE), MOE_BLOCK)
    return y.reshape(b, l, d)


def kernel(x_prompt, x_sample, attn_norm, w_in, q_norm, w_uq, kv_norm, w_ukv, lb_param, hg_norm, w_branch, w_out,
           ffn_norm, w_group, b_group, w_expert, b_expert, w_gate, w_up, w_down, final_norm):
    wts = _prep_weights(attn_norm, w_in, q_norm, w_uq, kv_norm, w_ukv, lb_param, hg_norm, w_branch, w_out,
                        ffn_norm, w_group, b_group, w_expert, b_expert, w_gate, w_up, w_down, final_norm)
    return (_encoder(x_prompt, wts), _encoder(x_sample, wts))
```

```python
import functools
import math

import jax
import jax.numpy as jnp
from jax import lax
from jax.experimental import pallas as pl
from jax.experimental.pallas import tpu as pltpu

D_MODEL = 1024
MLA_HEADS = 8
Q_LORA = 384
KV_LORA = 256
QK_NOPE = 64
QK_ROPE = 32
V_DIM = 64
QK_DIM = QK_NOPE + QK_ROPE
ROPE_THETA = 10000.0
HG_HEADS = 4
HG_DK = 128
HG_DV = 128
HG_KW = HG_HEADS * HG_DK
HG_WIDTH = HG_HEADS * HG_DV
N_GROUPS = 4
EXPERTS_PER_GROUP = 8
N_EXPERTS = N_GROUPS * EXPERTS_PER_GROUP
D_EXPERT = 256
GROUP_FF = EXPERTS_PER_GROUP * D_EXPERT
EPS = 1e-6

LANES = 128
SUBLANES = 8
SLAB_LEVEL = 3
HEAD_PAD = LANES
ATT_W = MLA_HEADS * HEAD_PAD
ATT_VROWS = V_DIM + 16
VMEM_LIMIT = 56 * 1024 * 1024

F32 = jnp.float32
BF16 = jnp.bfloat16


def _dot(a, b):
    return jnp.dot(a, b, preferred_element_type=F32)


def _dot_nt(a, b):
    return lax.dot_general(a, b, (((1,), (1,)), ((), ())), preferred_element_type=F32)


def _dot_tn(a, b):
    return lax.dot_general(a, b, (((0,), (0,)), ((), ())), preferred_element_type=F32)


def _rms(x, g):
    return x * lax.rsqrt(jnp.mean(x * x, axis=-1, keepdims=True) + EPS) * g


def _const_spec(shape):
    zeros = (0,) * len(shape)
    return pl.BlockSpec(shape, lambda *_: zeros, pipeline_mode=pl.Buffered(1))


def _params(sem):
    return pltpu.CompilerParams(dimension_semantics=sem, vmem_limit_bytes=VMEM_LIMIT)


def _in_proj_kernel(x_ref, cos_ref, sin_ref, an_ref, qn_ref, kvn_ref, w_lat_ref, w_kpe_ref, w_hg_ref,
                    w_gate_ref, w_uq_ref, w_uk_ref, w_uvt_ref,
                    q_ref, k_ref, vt_ref, hq_ref, hi_ref, fzf_ref, fzb_ref, hg_ref, ga_ref, gb_ref):
    h = _rms(x_ref[...], an_ref[...]).astype(BF16)
    cos = cos_ref[...]
    sin = sin_ref[...]

    lat = _dot(h, w_lat_ref[...])
    cqn = _rms(lat[:, :Q_LORA], qn_ref[...]).astype(BF16)
    ckvn = _rms(lat[:, Q_LORA:], kvn_ref[...]).astype(BF16)

    kp = _dot(h, w_kpe_ref[...])
    k_rot = kp[:, :HEAD_PAD] * cos + kp[:, HEAD_PAD:] * sin
    q2 = _dot(cqn, w_uq_ref[...])
    k_nope = _dot(ckvn, w_uk_ref[...])
    scale = QK_DIM ** -0.5 * math.log2(math.e)
    for hd in range(MLA_HEADS):
        lo, hi = hd * HEAD_PAD, (hd + 1) * HEAD_PAD
        q_ref[:, lo:hi] = ((q2[:, lo:hi] * cos + q2[:, ATT_W + lo:ATT_W + hi] * sin) * scale).astype(BF16)
        k_ref[:, lo:hi] = (k_nope[:, lo:hi] + k_rot).astype(BF16)
    v_t = _dot_nt(w_uvt_ref[...], ckvn)
    row = lax.broadcasted_iota(jnp.int32, (ATT_W, 1), 0)
    vt_ref[0, 0] = jnp.where(row % HEAD_PAD == V_DIM, 1.0, v_t).astype(BF16)

    hz = _dot(h, w_hg_ref[...])
    hq_ref[...] = (hz[:, :HG_KW] * (HG_DK ** -0.5)).astype(BF16)
    fzf_ref[...] = hz[:, HG_KW:2 * HG_KW]
    fzb_ref[...] = hz[:, 2 * HG_KW:3 * HG_KW]
    hi_ref[...] = hz[:, 3 * HG_KW:3 * HG_KW + HG_WIDTH].astype(BF16)
    hg_ref[...] = hz[:, 3 * HG_KW + HG_WIDTH:].astype(BF16)

    gz = _dot(h, w_gate_ref[...])
    ga_ref[...] = gz[:, :D_MODEL].astype(BF16)
    gb_ref[...] = gz[:, D_MODEL:].astype(BF16)


def _in_proj(x2d, seq_len, tabs, wts, tm):
    t = x2d.shape[0]
    nl = seq_len // tm
    row = lambda w: pl.BlockSpec((tm, w), lambda i: (i, 0))
    tab = pl.BlockSpec((tm, HEAD_PAD), lambda i: (i % nl, 0))
    consts = [wts['an'], wts['qn'], wts['kvn'], wts['w_lat'], wts['w_kpe'], wts['w_hg'], wts['w_gate'],
              wts['w_uq'], wts['w_uk'], wts['w_uvt']]
    out_w = [(ATT_W, BF16), (ATT_W, BF16), None, (HG_KW, BF16), (HG_WIDTH, BF16), (HG_KW, F32),
             (HG_KW, F32), (HG_WIDTH, BF16), (D_MODEL, BF16), (D_MODEL, BF16)]
    vt_spec = pl.BlockSpec((1, 1, ATT_W, tm), lambda i: (i // nl, i % nl, 0, 0))
    vt_shape = jax.ShapeDtypeStruct((t // seq_len, nl, ATT_W, tm), BF16)
    return pl.pallas_call(
        _in_proj_kernel,
        name="in_proj",
        grid=(t // tm,),
        in_specs=[row(D_MODEL), tab, tab] + [_const_spec(c.shape) for c in consts],
        out_specs=[vt_spec if o is None else row(o[0]) for o in out_w],
        out_shape=[vt_shape if o is None else jax.ShapeDtypeStruct((t, o[0]), o[1]) for o in out_w],
        compiler_params=_params(("parallel",)),
    )(x2d, tabs[0], tabs[1], *consts)


def _attention_kernel(q_ref, k_ref, vt_ref, o_ref, sa_ref, sb_ref, cmax_ref, m_ref, acc_ref, *, tk, unroll):
    q = q_ref[0]
    nk = k_ref.shape[1] // tk

    def scores(j):
        off = pl.multiple_of(j * tk, tk)
        return _dot_nt(k_ref[0, pl.ds(off, tk), :], q)

    bufs = (sa_ref, sb_ref)

    def fill(u, j):
        s = scores(j)
        bufs[u][...] = s
        cmax_ref[u:u + 1, :] = jnp.max(s, axis=0, keepdims=True)

    def absorb(u, j):
        m_old = m_ref[...]
        m_new = jnp.maximum(m_old, cmax_ref[u:u + 1, :])
        p = jnp.exp2(bufs[u][...] - m_new).astype(BF16)
        pv = _dot(vt_ref[0, j, :ATT_VROWS, :], p)
        acc_ref[:ATT_VROWS, :] = jnp.exp2(m_old - m_new) * acc_ref[:ATT_VROWS, :] + pv
        m_ref[...] = m_new

    m_ref[...] = jnp.full(m_ref.shape, -jnp.inf, F32)
    acc_ref[...] = jnp.zeros(acc_ref.shape, F32)
    fill(0, 0)

    def block(base, last):
        for u in range(unroll):
            if not (last and u == unroll - 1):
                fill((u + 1) % 2, base + u + 1)
            absorb(u % 2, base + u)

    def body(i, carry):
        block(i * unroll, False)
        return carry

    lax.fori_loop(0, nk // unroll - 1, body, 0)
    block(nk - unroll, True)
    acc = acc_ref[...]
    o_ref[0] = (acc / acc[V_DIM:V_DIM + 1, :]).T.astype(o_ref.dtype)


def _attention(q, k, vt, tq, tk):
    b, l, _ = q.shape
    nk = l // tk
    assert vt.shape == (b, nk, ATT_W, tk)
    vspec = pl.BlockSpec((1, nk, HEAD_PAD, tk), lambda bi, h, i: (bi, 0, h, 0))
    unroll = ATT_UNROLL if nk >= 3 * ATT_UNROLL else 2
    assert nk % unroll == 0
    qspec = pl.BlockSpec((1, tq, HEAD_PAD), lambda bi, h, i: (bi, i, h))
    kspec = pl.BlockSpec((1, l, HEAD_PAD), lambda bi, h, i: (bi, 0, h))
    return pl.pallas_call(
        functools.partial(_attention_kernel, tk=tk, unroll=unroll),
        name="attention",
        grid=(b, MLA_HEADS, l // tq),
        in_specs=[qspec, kspec, vspec],
        out_specs=qspec,
        out_shape=jax.ShapeDtypeStruct((b, l, ATT_W), BF16),
        scratch_shapes=[pltpu.VMEM((tk, tq), F32), pltpu.VMEM((tk, tq), F32), pltpu.VMEM((2, tq), F32),
                        pltpu.VMEM((1, tq), F32), pltpu.VMEM((HEAD_PAD, tq), F32)],
        compiler_params=_params(("parallel", "parallel", "arbitrary")),
    )(q, k, vt)


def _gla_masks(c, reverse):
    t = lax.broadcasted_iota(jnp.int32, (c, HG_DK), 0)
    tau = (c - 1 - t) if reverse else t
    uppers = [((tau >> j) & 1) == 1 for j in range(int(math.log2(c)))]
    row = lax.broadcasted_iota(jnp.int32, (c, c), 0)
    col = lax.broadcasted_iota(jnp.int32, (c, c), 1)
    lev = 31 - lax.clz(row ^ col)
    lev = jnp.where((row <= col) if reverse else (row >= col), lev, -2)
    lane = lax.broadcasted_iota(jnp.int32, (SUBLANES, c), 1)
    early = {}
    for j in range(SLAB_LEVEL, int(math.log2(c))):
        half = ((lane >> j) & 1) == (1 if reverse else 0)
        for b in range(c >> (j + 1)):
            early[j, b] = half & ((lane >> (j + 1)) == b)
    return uppers, lev, early


def _gla_chunk(q, v, z, lb, st_ref, masks, reverse):
    uppers, lev, early = masks
    c = q.shape[0]
    q = q.astype(F32)
    f = lb + (1.0 - lb) * jax.nn.sigmoid(z)
    kk = 1.0 - f

    def from_prev(a, m):
        return pltpu.roll(a, (c - m) if reverse else m, 0)

    def from_next(a, m):
        return pltpu.roll(a, m if reverse else (c - m), 0)

    kk16 = kk.astype(BF16)
    a_mat = jnp.where(lev == -1, _dot_nt(q.astype(BF16), kk16), 0.0)
    head = f
    tail = None
    tot = f
    for j, upper in enumerate(uppers[:SLAB_LEVEL]):
        m = 1 << j
        kj = kk16 if tail is None else (kk * tail).astype(BF16)
        a_mat = jnp.where(lev == j, _dot_nt((q * head).astype(BF16), kj), a_mat)
        sib = jnp.where(upper, from_prev(tot, m), from_next(tot, m))
        head = jnp.where(upper, head * sib, head)
        tail = jnp.where(upper, 1.0, sib) if tail is None else jnp.where(upper, tail, tail * sib)
        tot = tot * sib

    ns = c // SUBLANES
    slab = lambda a: [a[SUBLANES * i:SUBLANES * (i + 1)] for i in range(ns)]
    q_s, kk_s, head_s, tail_s, tot_s, a_s = slab(q), slab(kk), slab(head), slab(tail), slab(tot), slab(a_mat)
    zero16 = jnp.zeros((SUBLANES, HG_DK), BF16)
    for j in range(SLAB_LEVEL, len(uppers)):
        w = 1 << (j - SLAB_LEVEL)
        late, k_rows = [], []
        for b in range(ns // (2 * w)):
            first, second = range(2 * w * b, 2 * w * b + w), range(2 * w * b + w, 2 * w * (b + 1))
            lo, up = (second, first) if reverse else (first, second)
            late += [(i, b) for i in up]
            rows = {i: (kk_s[i] * tail_s[i]).astype(BF16) for i in lo}
            k_rows += [rows.get(i, zero16) for i in range(2 * w * b, 2 * w * (b + 1))]
        qc = jnp.concatenate([(q_s[i] * head_s[i]).astype(BF16) for i, _ in late], axis=0)
        pj = _dot_nt(qc, jnp.concatenate(k_rows, axis=0))
        for n, (i, b) in enumerate(late):
            a_s[i] = jnp.where(early[j, b], pj[SUBLANES * n:SUBLANES * (n + 1)], a_s[i])
        for b in range(ns // (2 * w)):
            first, second = range(2 * w * b, 2 * w * b + w), range(2 * w * b + w, 2 * w * (b + 1))
            lo, up = (second, first) if reverse else (first, second)
            t_lo, t_up = tot_s[lo[0]], tot_s[up[0]]
            for i in up:
                head_s[i] = head_s[i] * t_lo
            for i in lo:
                tail_s[i] = tail_s[i] * t_up
            both = t_lo * t_up
            for i in range(2 * w * b, 2 * w * (b + 1)):
                tot_s[i] = both
    a_mat = jnp.concatenate(a_s, axis=0)
    head = jnp.concatenate(head_s, axis=0)
    tail = jnp.concatenate(tail_s, axis=0)
    tot = tot_s[0]

    st = st_ref[...]
    o = _dot(a_mat.astype(BF16), v) + _dot_nt((q * head).astype(BF16), st.astype(BF16))
    st_ref[...] = st * tot[0:1, :] + _dot_tn(v, (kk * tail).astype(BF16))
    return o


def _gla_kernel(qf_ref, vf_ref, zf_ref, qb_ref, vb_ref, zb_ref, lb_ref, of_ref, ob_ref, sf_ref, sb_ref, *, chunk):
    @pl.when(pl.program_id(2) == 0)
    def _():
        sf_ref[...] = jnp.zeros_like(sf_ref)
        sb_ref[...] = jnp.zeros_like(sb_ref)

    tb = qf_ref.shape[1]
    n = tb // chunk
    lbp = lb_ref[...]
    lbe = jnp.exp(lbp - jnp.max(lbp, axis=0, keepdims=True))
    lb = lbe[0] / jnp.sum(lbe, axis=0)
    lbf = lb[0:1, :]
    lbb = lb[1:2, :]
    masks_f = _gla_masks(chunk, False)
    masks_b = _gla_masks(chunk, True)
    for c in range(n):
        sl = pl.ds(c * chunk, chunk)
        of_ref[0, sl, :] = _gla_chunk(qf_ref[0, sl, :], vf_ref[0, sl, :], zf_ref[0, sl, :], lbf, sf_ref,
                                      masks_f, False)
        sl = pl.ds((n - 1 - c) * chunk, chunk)
        ob_ref[0, sl, :] = _gla_chunk(qb_ref[0, sl, :], vb_ref[0, sl, :], zb_ref[0, sl, :], lbb, sb_ref,
                                      masks_b, True)


def _gla(hq, hi, fzf, fzb, lbs, tb, chunk):
    b, l, _ = hq.shape
    nt = l // tb
    fwd = pl.BlockSpec((1, tb, HG_DK), lambda bi, h, i: (bi, i, h))
    bwd = pl.BlockSpec((1, tb, HG_DK), lambda bi, h, i: (bi, nt - 1 - i, h))
    lbspec = pl.BlockSpec((lbs.shape[0], 2, HG_DK), lambda bi, h, i: (0, 0, h))
    out = jax.ShapeDtypeStruct((b, l, HG_WIDTH), F32)
    return pl.pallas_call(
        functools.partial(_gla_kernel, chunk=chunk),
        name="gla",
        grid=(b, HG_HEADS, nt),
        in_specs=[fwd, fwd, fwd, bwd, bwd, bwd, lbspec],
        out_specs=[fwd, bwd],
        out_shape=[out, out],
        scratch_shapes=[pltpu.VMEM((HG_DV, HG_DK), F32), pltpu.VMEM((HG_DV, HG_DK), F32)],
        compiler_params=_params(("parallel", "parallel", "arbitrary")),
    )(hq, hi, fzf, hq, hi, fzb, lbs)


def _route(logits):
    lane = lax.broadcasted_iota(jnp.int32, logits.shape, 1)
    neg = -jnp.inf
    big = jnp.int32(LANES)

    is_g = (lane >= N_EXPERTS) & (lane < N_EXPERTS + N_GROUPS)
    gl = jnp.where(is_g, logits, neg)
    ge = jnp.exp(gl - jnp.max(gl, axis=-1, keepdims=True))
    gp = ge / jnp.sum(ge, axis=-1, keepdims=True)
    g_val = jnp.max(gp, axis=-1, keepdims=True)
    g_idx = jnp.min(jnp.where(is_g & (gp == g_val), lane, big), axis=-1, keepdims=True) - N_EXPERTS

    sel = (lane < N_EXPERTS) & ((lane >> 3) == g_idx)
    el = jnp.where(sel, logits, neg)
    ee = jnp.exp(el - jnp.max(el, axis=-1, keepdims=True))
    ep = ee / jnp.sum(ee, axis=-1, keepdims=True)
    v1 = jnp.max(ep, axis=-1, keepdims=True)
    i1 = jnp.min(jnp.where(sel & (ep == v1), lane, big), axis=-1, keepdims=True)
    rest = jnp.where(sel & (lane != i1), ep, -1.0)
    v2 = jnp.max(rest, axis=-1, keepdims=True)
    i2 = jnp.min(jnp.where(rest == v2, lane, big), axis=-1, keepdims=True)
    den = v1 + v2
    comb = jnp.where(lane == i1, g_val * v1 / den, 0.0) + jnp.where(lane == i2, g_val * v2 / den, 0.0)
    return comb + jnp.where(lane == N_EXPERTS, g_idx.astype(F32), 0.0)


def _merge_kernel(x_ref, oa_ref, of_ref, ob_ref, hg_ref, ga_ref, gb_ref, hgn_ref, ffn_ref, wb0_ref, wb1_ref,
                  wout_ref, wr_ref, br_ref, x1_ref, h2_ref, comb_ref):
    o = of_ref[...] + ob_ref[...]
    hgn = hgn_ref[...]
    parts = []
    for hd in range(HG_HEADS):
        lo, hi = hd * HG_DV, (hd + 1) * HG_DV
        parts.append(_rms(o[:, lo:hi], hgn[:, lo:hi]))
    o_b = (jnp.concatenate(parts, axis=-1) * jax.nn.silu(hg_ref[...].astype(F32))).astype(BF16)
    merged = (jax.nn.sigmoid(ga_ref[...].astype(F32)) * _dot(oa_ref[...], wb0_ref[...])
              + jax.nn.sigmoid(gb_ref[...].astype(F32)) * _dot(o_b, wb1_ref[...]))
    x1 = x_ref[...] + _dot(merged.astype(BF16), wout_ref[...])
    x1_ref[...] = x1
    h2 = _rms(x1, ffn_ref[...])
    h2_hi = h2.astype(BF16)
    h2_ref[...] = h2_hi
    h2_lo = (h2 - h2_hi.astype(F32)).astype(BF16)
    r = _dot(h2_hi, wr_ref[...])
    logits = r[:, :LANES] + r[:, LANES:] + _dot(h2_lo, wr_ref[:, :LANES]) + br_ref[...]
    comb_ref[...] = _route(logits)


def _merge(x2d, oa, of, ob, hg, ga, gb, wts, tm):
    t = x2d.shape[0]
    row = lambda w: pl.BlockSpec((tm, w), lambda i: (i, 0))
    consts = [wts['hgn'], wts['ffn'], wts['wb0'], wts['wb1'], wts['w_out'], wts['w_router'], wts['b_router']]
    return pl.pallas_call(
        _merge_kernel,
        name="merge",
        grid=(t // tm,),
        in_specs=[row(D_MODEL), row(ATT_W), row(HG_WIDTH), row(HG_WIDTH), row(HG_WIDTH), row(D_MODEL),
                  row(D_MODEL)] + [_const_spec(c.shape) for c in consts],
        out_specs=[row(D_MODEL), row(D_MODEL), row(LANES)],
        out_shape=[jax.ShapeDtypeStruct((t, D_MODEL), F32), jax.ShapeDtypeStruct((t, D_MODEL), BF16),
                   jax.ShapeDtypeStruct((t, LANES), F32)],
        compiler_params=_params(("parallel",)),
    )(x2d, oa, of, ob, hg, ga, gb, *consts)


def _moe_kernel(x1_ref, h2_ref, comb_ref, wg_ref, wu_ref, wd_ref, fn_ref, y_ref,
                acc_ref, pos_ref, cw_ref, tri_ref, hh_ref, nblk_ref, *, blk):
    tile = pl.program_id(0)
    grp = pl.program_id(1)
    tm = h2_ref.shape[0]

    @pl.when((tile == 0) & (grp == 0))
    def _():
        r = lax.broadcasted_iota(jnp.int32, (tm, tm), 0)
        c = lax.broadcasted_iota(jnp.int32, (tm, tm), 1)
        tri_ref[...] = jnp.where(r < c, 1.0, 0.0).astype(BF16)

    @pl.when(grp == 0)
    def _():
        acc_ref[...] = jnp.zeros_like(acc_ref)
        comb = comb_ref[...]
        hi = comb.astype(BF16)
        cw_ref[:, :LANES] = hi
        cw_ref[:, LANES:] = (comb - hi.astype(F32)).astype(BF16)
        gid = comb.T[N_EXPERTS:N_EXPERTS + 1, :].astype(jnp.int32)
        g8 = lax.broadcasted_iota(jnp.int32, (8, tm), 0)
        onehot = g8 == gid
        ones = jnp.where(onehot, 1.0, 0.0)
        rank = _dot(ones.astype(BF16), tri_ref[...])
        nblk = (jnp.sum(ones, axis=1, keepdims=True).astype(jnp.int32) + (blk - 1)) >> int(math.log2(blk))
        start = jnp.zeros((8, 1), jnp.int32)
        run = jnp.zeros((1, 1), jnp.int32)
        g81 = lax.broadcasted_iota(jnp.int32, (8, 1), 0)
        for k in range(N_GROUPS):
            start = jnp.where(g81 == k, run, start)
            nblk_ref[0, k] = nblk[k, 0]
            nblk_ref[1, k] = run[0, 0]
            run = run + nblk[k:k + 1, :]
        pos = jnp.sum(jnp.where(onehot, rank + (start * blk).astype(F32), 0.0), axis=0, keepdims=True)
        pos_ref[...] = pos.astype(jnp.int32)

    first = nblk_ref[1, grp]

    def block(b, carry):
        rows = (first + b) * blk + lax.broadcasted_iota(jnp.int32, (blk, tm), 0)
        p = jnp.where(rows == pos_ref[...], 1.0, 0.0).astype(BF16)
        xs = _dot(p, h2_ref[...]).astype(BF16)
        cw2 = _dot(p, cw_ref[...])
        cw = cw2[:, :LANES] + cw2[:, LANES:]
        gate = _dot(xs, wg_ref[0])
        up = _dot(xs, wu_ref[0])
        lane = lax.broadcasted_iota(jnp.int32, cw.shape, 1)
        for e in range(EXPERTS_PER_GROUP):
            lo, hi = e * D_EXPERT, (e + 1) * D_EXPERT
            w_e = jnp.sum(jnp.where(lane == grp * EXPERTS_PER_GROUP + e, cw, 0.0), axis=-1, keepdims=True)
            hh_ref[:, lo:hi] = (jax.nn.silu(gate[:, lo:hi]) * up[:, lo:hi] * w_e).astype(BF16)
        yb = _dot(hh_ref[...], wd_ref[0]).astype(BF16)
        acc_ref[...] += _dot_tn(p, yb)
        return carry

    lax.fori_loop(0, nblk_ref[0, grp], block, 0)

    @pl.when(grp == N_GROUPS - 1)
    def _():
        y_ref[...] = _rms(x1_ref[...] + acc_ref[...], fn_ref[...])


def _moe(x1, h2, comb, wts, tm, blk):
    t = x1.shape[0]
    row = lambda w: pl.BlockSpec((tm, w), lambda i, g: (i, 0))
    return pl.pallas_call(
        functools.partial(_moe_kernel, blk=blk),
        name="moe",
        grid=(t // tm, N_GROUPS),
        in_specs=[row(D_MODEL), row(D_MODEL), row(LANES),
                  pl.BlockSpec((1, D_MODEL, GROUP_FF), lambda i, g: (g, 0, 0)),
                  pl.BlockSpec((1, D_MODEL, GROUP_FF), lambda i, g: (g, 0, 0)),
                  pl.BlockSpec((1, GROUP_FF, D_MODEL), lambda i, g: (g, 0, 0)),
                  pl.BlockSpec((1, D_MODEL), lambda i, g: (0, 0))],
        out_specs=row(D_MODEL),
        out_shape=jax.ShapeDtypeStruct((t, D_MODEL), F32),
        scratch_shapes=[pltpu.VMEM((tm, D_MODEL), F32), pltpu.VMEM((1, tm), jnp.int32),
                        pltpu.VMEM((tm, 2 * LANES), BF16), pltpu.VMEM((tm, tm), BF16),
                        pltpu.VMEM((blk, GROUP_FF), BF16), pltpu.SMEM((2, N_GROUPS), jnp.int32)],
        compiler_params=_params(("arbitrary", "arbitrary")),
    )(x1, h2, comb, wts['w_gate_e'], wts['w_up_e'], wts['w_down_e'], wts['fn'])


def _rope_tables(seq_len):
    pos = jnp.arange(seq_len, dtype=F32)
    inv_freq = 1.0 / (ROPE_THETA ** (jnp.arange(0, QK_ROPE, 2, dtype=F32) / QK_ROPE))
    ang = pos[:, None] * inv_freq[None, :]
    cos, sin = jnp.cos(ang), jnp.sin(ang)
    pad = jnp.zeros((seq_len, HEAD_PAD - QK_DIM), F32)
    cos_t = jnp.concatenate([jnp.ones((seq_len, QK_NOPE), F32), cos, cos, pad], axis=-1)
    sin_t = jnp.concatenate([jnp.zeros((seq_len, QK_NOPE), F32), sin, sin, pad], axis=-1)
    return cos_t, sin_t


def _rot_half_cols(w):
    half = QK_ROPE // 2
    return jnp.concatenate([-w[..., half:], w[..., :half]], axis=-1)


def _prep_weights(attn_norm, w_in, q_norm, w_uq, kv_norm, w_ukv, lb_param, hg_norm, w_branch, w_out, ffn_norm,
                  w_group, b_group, w_expert, b_expert, w_gate, w_up, w_down, final_norm):
    w_in = w_in[0]
    o_kpe = Q_LORA + KV_LORA
    o_hg = o_kpe + QK_ROPE
    o_gate = o_hg + 3 * HG_KW + 2 * HG_WIDTH
    w_kpe = w_in[:, o_kpe:o_hg]
    zpad = lambda n: jnp.zeros((D_MODEL, n), F32)
    place = lambda w: jnp.concatenate([zpad(QK_NOPE), w, zpad(HEAD_PAD - QK_DIM)], axis=-1)
    w_kpe2 = jnp.concatenate([place(w_kpe), place(_rot_half_cols(w_kpe))], axis=-1)

    uq = w_uq[0]
    zq = lambda n: jnp.zeros((Q_LORA, MLA_HEADS, n), F32)
    uq_full = jnp.concatenate([uq, zq(HEAD_PAD - QK_DIM)], axis=-1)
    uq_rot = jnp.concatenate([zq(QK_NOPE), _rot_half_cols(uq[..., QK_NOPE:]), zq(HEAD_PAD - QK_DIM)], axis=-1)
    w_uq2 = jnp.concatenate([uq_full.reshape(Q_LORA, ATT_W), uq_rot.reshape(Q_LORA, ATT_W)], axis=-1)

    ukv = w_ukv[0]
    zk = lambda n: jnp.zeros((KV_LORA, MLA_HEADS, n), F32)
    uk = jnp.concatenate([ukv[..., :QK_NOPE], zk(HEAD_PAD - QK_NOPE)], axis=-1)
    uv = jnp.concatenate([ukv[..., QK_NOPE:], zk(HEAD_PAD - V_DIM)], axis=-1)
    w_uk = uk.reshape(KV_LORA, ATT_W)
    w_uvt = uv.reshape(KV_LORA, ATT_W).T

    wb0 = w_branch[0, 0].reshape(MLA_HEADS, V_DIM, D_MODEL)
    wb0 = jnp.concatenate([wb0, jnp.zeros((MLA_HEADS, HEAD_PAD - V_DIM, D_MODEL), F32)], axis=1)

    w_router = jnp.concatenate([w_expert[0], w_group[0],
                                jnp.zeros((D_MODEL, LANES - N_EXPERTS - N_GROUPS), F32)], axis=-1)
    b_router = jnp.concatenate([b_expert[0], b_group[0], jnp.zeros((LANES - N_EXPERTS - N_GROUPS,), F32)])
    w_router_hi = w_router.astype(BF16)
    w_router_lo = (w_router - w_router_hi.astype(F32)).astype(BF16)
    w_router = jnp.concatenate([w_router_hi, w_router_lo], axis=-1)

    ff = lambda w: jnp.transpose(w[0], (0, 2, 1, 3)).reshape(N_GROUPS, D_MODEL, GROUP_FF).astype(BF16)
    lbs = lb_param.astype(F32)
    return {
        'an': attn_norm[0][None], 'qn': q_norm[0][None], 'kvn': kv_norm[0][None],
        'w_lat': w_in[:, :o_kpe].astype(BF16), 'w_kpe': w_kpe2.astype(BF16),
        'w_hg': w_in[:, o_hg:o_gate].astype(BF16), 'w_gate': w_in[:, o_gate:].astype(BF16),
        'w_uq': w_uq2.astype(BF16), 'w_uk': w_uk.astype(BF16), 'w_uvt': w_uvt.astype(BF16), 'lbs': lbs,
        'hgn': hg_norm[0][None], 'ffn': ffn_norm[0][None],
        'wb0': wb0.reshape(ATT_W, D_MODEL).astype(BF16), 'wb1': w_branch[0, 1].astype(BF16),
        'w_out': w_out[0].astype(BF16), 'w_router': w_router, 'b_router': b_router[None],
        'w_gate_e': ff(w_gate), 'w_up_e': ff(w_up),
        'w_down_e': w_down[0].reshape(N_GROUPS, GROUP_FF, D_MODEL).astype(BF16), 'fn': final_norm[None],
    }


TOKEN_TILE = 512
ATT_TQ = 1024
ATT_TK = 512
ATT_UNROLL = 4
GLA_ROWS = 1024
GLA_CHUNK = 128
MOE_TILE = 1024
MOE_BLOCK = 128


def _tile(n, pref):
    t = min(n, pref)
    assert n % t == 0
    return t


def _encoder(x, wts):
    b, l, d = x.shape
    t = b * l
    x2d = x.reshape(t, d)
    tm = _tile(l, TOKEN_TILE)
    assert tm == _tile(l, ATT_TK)
    q, k, vt, hq, hi, fzf, fzb, hg, ga, gb = _in_proj(x2d, l, _rope_tables(l), wts, tm)
    r3 = lambda a: a.reshape(b, l, a.shape[-1])
    oa = _attention(r3(q), r3(k), vt, _tile(l, ATT_TQ), tm)
    of, ob = _gla(r3(hq), r3(hi), r3(fzf), r3(fzb), wts['lbs'], _tile(l, GLA_ROWS), GLA_CHUNK)
    x1, h2, comb = _merge(x2d, oa.reshape(t, ATT_W), of.reshape(t, HG_WIDTH), ob.reshape(t, HG_WIDTH),
                          hg, ga, gb, wts, tm)
    y = _moe(x1, h2, comb, wts, _tile(l, MOE_TILE), MOE_BLOCK)
    return y.reshape(b, l, d)


def kernel(x_prompt, x_sample, attn_norm, w_in, q_norm, w_uq, kv_norm, w_ukv, lb_param, hg_norm, w_branch, w_out,
           ffn_norm, w_group, b_group, w_expert, b_expert, w_gate, w_up, w_down, final_norm):
    wts = _prep_weights(attn_norm, w_in, q_norm, w_uq, kv_norm, w_ukv, lb_param, hg_norm, w_branch, w_out,
                        ffn_norm, w_group, b_group, w_expert, b_expert, w_gate, w_up, w_down, final_norm)
    return (_encoder(x_prompt, wts), _encoder(x_sample, wts))
```

```python
import functools
import math

import jax
import jax.numpy as jnp
from jax import lax
from jax.experimental import pallas as pl
from jax.experimental.pallas import tpu as pltpu

D_MODEL = 1024
MLA_HEADS = 8
Q_LORA = 384
KV_LORA = 256
QK_NOPE = 64
QK_ROPE = 32
V_DIM = 64
QK_DIM = QK_NOPE + QK_ROPE
ROPE_THETA = 10000.0
HG_HEADS = 4
HG_DK = 128
HG_DV = 128
HG_KW = HG_HEADS * HG_DK
HG_WIDTH = HG_HEADS * HG_DV
N_GROUPS = 4
EXPERTS_PER_GROUP = 8
N_EXPERTS = N_GROUPS * EXPERTS_PER_GROUP
D_EXPERT = 256
GROUP_FF = EXPERTS_PER_GROUP * D_EXPERT
EPS = 1e-6

LANES = 128
SUBLANES = 8
SLAB_LEVEL = 3
HEAD_PAD = LANES
ATT_W = MLA_HEADS * HEAD_PAD
ATT_VROWS = V_DIM + 16
VMEM_LIMIT = 56 * 1024 * 1024

F32 = jnp.float32
BF16 = jnp.bfloat16


def _dot(a, b):
    return jnp.dot(a, b, preferred_element_type=F32)


def _dot_nt(a, b):
    return lax.dot_general(a, b, (((1,), (1,)), ((), ())), preferred_element_type=F32)


def _dot_tn(a, b):
    return lax.dot_general(a, b, (((0,), (0,)), ((), ())), preferred_element_type=F32)


def _rms(x, g):
    return x * lax.rsqrt(jnp.mean(x * x, axis=-1, keepdims=True) + EPS) * g


def _const_spec(shape):
    zeros = (0,) * len(shape)
    return pl.BlockSpec(shape, lambda *_: zeros, pipeline_mode=pl.Buffered(1))


def _params(sem):
    return pltpu.CompilerParams(dimension_semantics=sem, vmem_limit_bytes=VMEM_LIMIT)


def _in_proj_kernel(x_ref, cos_ref, sin_ref, an_ref, qn_ref, kvn_ref, w_lat_ref, w_kpe_ref, w_hg_ref,
                    w_gate_ref, w_uq_ref, w_uk_ref, w_uvt_ref,
                    q_ref, k_ref, vt_ref, hq_ref, hi_ref, fzf_ref, fzb_ref, hg_ref, ga_ref, gb_ref):
    h = _rms(x_ref[...], an_ref[...]).astype(BF16)
    cos = cos_ref[...]
    sin = sin_ref[...]

    lat = _dot(h, w_lat_ref[...])
    cqn = _rms(lat[:, :Q_LORA], qn_ref[...]).astype(BF16)
    ckvn = _rms(lat[:, Q_LORA:], kvn_ref[...]).astype(BF16)

    kp = _dot(h, w_kpe_ref[...])
    k_rot = kp[:, :HEAD_PAD] * cos + kp[:, HEAD_PAD:] * sin
    q2 = _dot(cqn, w_uq_ref[...])
    k_nope = _dot(ckvn, w_uk_ref[...])
    scale = QK_DIM ** -0.5 * math.log2(math.e)
    for hd in range(MLA_HEADS):
        lo, hi = hd * HEAD_PAD, (hd + 1) * HEAD_PAD
        q_ref[:, lo:hi] = ((q2[:, lo:hi] * cos + q2[:, ATT_W + lo:ATT_W + hi] * sin) * scale).astype(BF16)
        k_ref[:, lo:hi] = (k_nope[:, lo:hi] + k_rot).astype(BF16)
    v_t = _dot_nt(w_uvt_ref[...], ckvn)
    row = lax.broadcasted_iota(jnp.int32, (ATT_W, 1), 0)
    vt_ref[0, 0] = jnp.where(row % HEAD_PAD == V_DIM, 1.0, v_t).astype(BF16)

    hz = _dot(h, w_hg_ref[...])
    hq_ref[...] = (hz[:, :HG_KW] * (HG_DK ** -0.5)).astype(BF16)
    fzf_ref[...] = hz[:, HG_KW:2 * HG_KW]
    fzb_ref[...] = hz[:, 2 * HG_KW:3 * HG_KW]
    hi_ref[...] = hz[:, 3 * HG_KW:3 * HG_KW + HG_WIDTH].astype(BF16)
    hg_ref[...] = hz[:, 3 * HG_KW + HG_WIDTH:].astype(BF16)

    gz = _dot(h, w_gate_ref[...])
    ga_ref[...] = gz[:, :D_MODEL].astype(BF16)
    gb_ref[...] = gz[:, D_MODEL:].astype(BF16)


def _in_proj(x2d, seq_len, tabs, wts, tm):
    t = x2d.shape[0]
    nl = seq_len // tm
    row = lambda w: pl.BlockSpec((tm, w), lambda i: (i, 0))
    tab = pl.BlockSpec((tm, HEAD_PAD), lambda i: (i % nl, 0))
    consts = [wts['an'], wts['qn'], wts['kvn'], wts['w_lat'], wts['w_kpe'], wts['w_hg'], wts['w_gate'],
              wts['w_uq'], wts['w_uk'], wts['w_uvt']]
    out_w = [(ATT_W, BF16), (ATT_W, BF16), None, (HG_KW, BF16), (HG_WIDTH, BF16), (HG_KW, F32),
             (HG_KW, F32), (HG_WIDTH, BF16), (D_MODEL, BF16), (D_MODEL, BF16)]
    vt_spec = pl.BlockSpec((1, 1, ATT_W, tm), lambda i: (i // nl, i % nl, 0, 0))
    vt_shape = jax.ShapeDtypeStruct((t // seq_len, nl, ATT_W, tm), BF16)
    return pl.pallas_call(
        _in_proj_kernel,
        name="in_proj",
        grid=(t // tm,),
        in_specs=[row(D_MODEL), tab, tab] + [_const_spec(c.shape) for c in consts],
        out_specs=[vt_spec if o is None else row(o[0]) for o in out_w],
        out_shape=[vt_shape if o is None else jax.ShapeDtypeStruct((t, o[0]), o[1]) for o in out_w],
        compiler_params=_params(("parallel",)),
    )(x2d, tabs[0], tabs[1], *consts)


def _attention_kernel(q_ref, k_ref, vt_ref, o_ref, sa_ref, sb_ref, cmax_ref, m_ref, acc_ref, *, tk, unroll):
    q = q_ref[0]
    nk = k_ref.shape[1] // tk

    def scores(j):
        off = pl.multiple_of(j * tk, tk)
        return _dot_nt(k_ref[0, pl.ds(off, tk), :], q)

    bufs = (sa_ref, sb_ref)

    def fill(u, j):
        s = scores(j)
        bufs[u][...] = s
        cmax_ref[u:u + 1, :] = jnp.max(s, axis=0, keepdims=True)

    def absorb(u, j):
        m_old = m_ref[...]
        m_new = jnp.maximum(m_old, cmax_ref[u:u + 1, :])
        p = jnp.exp2(bufs[u][...] - m_new).astype(BF16)
        pv = _dot(vt_ref[0, j, :ATT_VROWS, :], p)
        acc_ref[:ATT_VROWS, :] = jnp.exp2(m_old - m_new) * acc_ref[:ATT_VROWS, :] + pv
        m_ref[...] = m_new

    m_ref[...] = jnp.full(m_ref.shape, -jnp.inf, F32)
    acc_ref[...] = jnp.zeros(acc_ref.shape, F32)
    fill(0, 0)

    def block(base, last):
        for u in range(unroll):
            if not (last and u == unroll - 1):
                fill((u + 1) % 2, base + u + 1)
            absorb(u % 2, base + u)

    def body(i, carry):
        block(i * unroll, False)
        return carry

    lax.fori_loop(0, jnp.minimum(pl.program_id(2), 0) + (nk // unroll - 1), body, 0)
    block(nk - unroll, True)
    acc = acc_ref[...]
    o_ref[0] = (acc / acc[V_DIM:V_DIM + 1, :]).T.astype(o_ref.dtype)


def _attention(q, k, vt, tq, tk):
    b, l, _ = q.shape
    nk = l // tk
    assert vt.shape == (b, nk, ATT_W, tk)
    vspec = pl.BlockSpec((1, nk, HEAD_PAD, tk), lambda bi, h, i: (bi, 0, h, 0))
    unroll = ATT_UNROLL if nk >= 2 * ATT_UNROLL else 2
    assert nk % unroll == 0
    qspec = pl.BlockSpec((1, tq, HEAD_PAD), lambda bi, h, i: (bi, i, h))
    kspec = pl.BlockSpec((1, l, HEAD_PAD), lambda bi, h, i: (bi, 0, h))
    return pl.pallas_call(
        functools.partial(_attention_kernel, tk=tk, unroll=unroll),
        name="attention",
        grid=(b, MLA_HEADS, l // tq),
        in_specs=[qspec, kspec, vspec],
        out_specs=qspec,
        out_shape=jax.ShapeDtypeStruct((b, l, ATT_W), BF16),
        scratch_shapes=[pltpu.VMEM((tk, tq), F32), pltpu.VMEM((tk, tq), F32), pltpu.VMEM((2, tq), F32),
                        pltpu.VMEM((1, tq), F32), pltpu.VMEM((HEAD_PAD, tq), F32)],
        compiler_params=_params(("parallel", "parallel", "arbitrary")),
    )(q, k, vt)


def _gla_masks(c, reverse):
    t = lax.broadcasted_iota(jnp.int32, (c, HG_DK), 0)
    tau = (c - 1 - t) if reverse else t
    uppers = [((tau >> j) & 1) == 1 for j in range(int(math.log2(c)))]
    row = lax.broadcasted_iota(jnp.int32, (c, c), 0)
    col = lax.broadcasted_iota(jnp.int32, (c, c), 1)
    lev = 31 - lax.clz(row ^ col)
    lev = jnp.where((row <= col) if reverse else (row >= col), lev, -2)
    lane = lax.broadcasted_iota(jnp.int32, (SUBLANES, c), 1)
    early = {}
    for j in range(SLAB_LEVEL, int(math.log2(c))):
        half = ((lane >> j) & 1) == (1 if reverse else 0)
        for b in range(c >> (j + 1)):
            early[j, b] = half & ((lane >> (j + 1)) == b)
    return uppers, lev, early


def _gla_chunk(q, v, z, lb, st_ref, masks, reverse):
    uppers, lev, early = masks
    c = q.shape[0]
    q = q.astype(F32)
    f = lb + (1.0 - lb) * jax.nn.sigmoid(z)
    kk = 1.0 - f

    def from_prev(a, m):
        return pltpu.roll(a, (c - m) if reverse else m, 0)

    def from_next(a, m):
        return pltpu.roll(a, m if reverse else (c - m), 0)

    kk16 = kk.astype(BF16)
    a_mat = jnp.where(lev == -1, _dot_nt(q.astype(BF16), kk16), 0.0)
    head = f
    tail = None
    tot = f
    for j, upper in enumerate(uppers[:SLAB_LEVEL]):
        m = 1 << j
        kj = kk16 if tail is None else (kk * tail).astype(BF16)
        a_mat = jnp.where(lev == j, _dot_nt((q * head).astype(BF16), kj), a_mat)
        sib = jnp.where(upper, from_prev(tot, m), from_next(tot, m))
        head = jnp.where(upper, head * sib, head)
        tail = jnp.where(upper, 1.0, sib) if tail is None else jnp.where(upper, tail, tail * sib)
        tot = tot * sib

    ns = c // SUBLANES
    slab = lambda a: [a[SUBLANES * i:SUBLANES * (i + 1)] for i in range(ns)]
    q_s, kk_s, head_s, tail_s, tot_s, a_s = slab(q), slab(kk), slab(head), slab(tail), slab(tot), slab(a_mat)
    zero16 = jnp.zeros((SUBLANES, HG_DK), BF16)
    for j in range(SLAB_LEVEL, len(uppers)):
        w = 1 << (j - SLAB_LEVEL)
        late, k_rows = [], []
        for b in range(ns // (2 * w)):
            first, second = range(2 * w * b, 2 * w * b + w), range(2 * w * b + w, 2 * w * (b + 1))
            lo, up = (second, first) if reverse else (first, second)
            late += [(i, b) for i in up]
            rows = {i: (kk_s[i] * tail_s[i]).astype(BF16) for i in lo}
            k_rows += [rows.get(i, zero16) for i in range(2 * w * b, 2 * w * (b + 1))]
        qc = jnp.concatenate([(q_s[i] * head_s[i]).astype(BF16) for i, _ in late], axis=0)
        pj = _dot_nt(qc, jnp.concatenate(k_rows, axis=0))
        for n, (i, b) in enumerate(late):
            a_s[i] = jnp.where(early[j, b], pj[SUBLANES * n:SUBLANES * (n + 1)], a_s[i])
        for b in range(ns // (2 * w)):
            first, second = range(2 * w * b, 2 * w * b + w), range(2 * w * b + w, 2 * w * (b + 1))
            lo, up = (second, first) if reverse else (first, second)
            t_lo, t_up = tot_s[lo[0]], tot_s[up[0]]
            for i in up:
                head_s[i] = head_s[i] * t_lo
            for i in lo:
                tail_s[i] = tail_s[i] * t_up
            both = t_lo * t_up
            for i in range(2 * w * b, 2 * w * (b + 1)):
                tot_s[i] = both
    a_mat = jnp.concatenate(a_s, axis=0)
    head = jnp.concatenate(head_s, axis=0)
    tail = jnp.concatenate(tail_s, axis=0)
    tot = tot_s[0]

    st = st_ref[...]
    o = _dot(a_mat.astype(BF16), v) + _dot_nt((q * head).astype(BF16), st.astype(BF16))
    st_ref[...] = st * tot[0:1, :] + _dot_tn(v, (kk * tail).astype(BF16))
    return o


def _gla_kernel(qf_ref, vf_ref, zf_ref, qb_ref, vb_ref, zb_ref, lb_ref, of_ref, ob_ref, sf_ref, sb_ref, *, chunk):
    @pl.when(pl.program_id(2) == 0)
    def _():
        sf_ref[...] = jnp.zeros_like(sf_ref)
        sb_ref[...] = jnp.zeros_like(sb_ref)

    tb = qf_ref.shape[1]
    n = tb // chunk
    lbp = lb_ref[...]
    lbe = jnp.exp(lbp - jnp.max(lbp, axis=0, keepdims=True))
    lb = lbe[0] / jnp.sum(lbe, axis=0)
    lbf = lb[0:1, :]
    lbb = lb[1:2, :]
    masks_f = _gla_masks(chunk, False)
    masks_b = _gla_masks(chunk, True)
    for c in range(n):
        sl = pl.ds(c * chunk, chunk)
        of_ref[0, sl, :] = _gla_chunk(qf_ref[0, sl, :], vf_ref[0, sl, :], zf_ref[0, sl, :], lbf, sf_ref,
                                      masks_f, False)
        sl = pl.ds((n - 1 - c) * chunk, chunk)
        ob_ref[0, sl, :] = _gla_chunk(qb_ref[0, sl, :], vb_ref[0, sl, :], zb_ref[0, sl, :], lbb, sb_ref,
                                      masks_b, True)


def _gla(hq, hi, fzf, fzb, lbs, tb, chunk):
    b, l, _ = hq.shape
    nt = l // tb
    fwd = pl.BlockSpec((1, tb, HG_DK), lambda bi, h, i: (bi, i, h))
    bwd = pl.BlockSpec((1, tb, HG_DK), lambda bi, h, i: (bi, nt - 1 - i, h))
    lbspec = pl.BlockSpec((lbs.shape[0], 2, HG_DK), lambda bi, h, i: (0, 0, h))
    out = jax.ShapeDtypeStruct((b, l, HG_WIDTH), F32)
    return pl.pallas_call(
        functools.partial(_gla_kernel, chunk=chunk),
        name="gla",
        grid=(b, HG_HEADS, nt),
        in_specs=[fwd, fwd, fwd, bwd, bwd, bwd, lbspec],
        out_specs=[fwd, bwd],
        out_shape=[out, out],
        scratch_shapes=[pltpu.VMEM((HG_DV, HG_DK), F32), pltpu.VMEM((HG_DV, HG_DK), F32)],
        compiler_params=_params(("parallel", "parallel", "arbitrary")),
    )(hq, hi, fzf, hq, hi, fzb, lbs)


def _route(logits):
    lane = lax.broadcasted_iota(jnp.int32, logits.shape, 1)
    neg = -jnp.inf
    big = jnp.int32(LANES)

    is_g = (lane >= N_EXPERTS) & (lane < N_EXPERTS + N_GROUPS)
    gl = jnp.where(is_g, logits, neg)
    ge = jnp.exp(gl - jnp.max(gl, axis=-1, keepdims=True))
    gp = ge / jnp.sum(ge, axis=-1, keepdims=True)
    g_val = jnp.max(gp, axis=-1, keepdims=True)
    g_idx = jnp.min(jnp.where(is_g & (gp == g_val), lane, big), axis=-1, keepdims=True) - N_EXPERTS

    sel = (lane < N_EXPERTS) & ((lane >> 3) == g_idx)
    el = jnp.where(sel, logits, neg)
    ee = jnp.exp(el - jnp.max(el, axis=-1, keepdims=True))
    ep = ee / jnp.sum(ee, axis=-1, keepdims=True)
    v1 = jnp.max(ep, axis=-1, keepdims=True)
    i1 = jnp.min(jnp.where(sel & (ep == v1), lane, big), axis=-1, keepdims=True)
    rest = jnp.where(sel & (lane != i1), ep, -1.0)
    v2 = jnp.max(rest, axis=-1, keepdims=True)
    i2 = jnp.min(jnp.where(rest == v2, lane, big), axis=-1, keepdims=True)
    den = v1 + v2
    comb = jnp.where(lane == i1, g_val * v1 / den, 0.0) + jnp.where(lane == i2, g_val * v2 / den, 0.0)
    return comb + jnp.where(lane == N_EXPERTS, g_idx.astype(F32), 0.0)


def _merge_kernel(x_ref, oa_ref, of_ref, ob_ref, hg_ref, ga_ref, gb_ref, hgn_ref, ffn_ref, wb0_ref, wb1_ref,
                  wout_ref, wr_ref, br_ref, x1_ref, h2_ref, comb_ref):
    o = of_ref[...] + ob_ref[...]
    hgn = hgn_ref[...]
    parts = []
    for hd in range(HG_HEADS):
        lo, hi = hd * HG_DV, (hd + 1) * HG_DV
        parts.append(_rms(o[:, lo:hi], hgn[:, lo:hi]))
    o_b = (jnp.concatenate(parts, axis=-1) * jax.nn.silu(hg_ref[...].astype(F32))).astype(BF16)
    merged = (jax.nn.sigmoid(ga_ref[...].astype(F32)) * _dot(oa_ref[...], wb0_ref[...])
              + jax.nn.sigmoid(gb_ref[...].astype(F32)) * _dot(o_b, wb1_ref[...]))
    x1 = x_ref[...] + _dot(merged.astype(BF16), wout_ref[...])
    x1_ref[...] = x1
    h2 = _rms(x1, ffn_ref[...])
    h2_hi = h2.astype(BF16)
    h2_ref[...] = h2_hi
    h2_lo = (h2 - h2_hi.astype(F32)).astype(BF16)
    r = _dot(h2_hi, wr_ref[...])
    logits = r[:, :LANES] + r[:, LANES:] + _dot(h2_lo, wr_ref[:, :LANES]) + br_ref[...]
    comb_ref[...] = _route(logits)


def _merge(x2d, oa, of, ob, hg, ga, gb, wts, tm):
    t = x2d.shape[0]
    row = lambda w: pl.BlockSpec((tm, w), lambda i: (i, 0))
    consts = [wts['hgn'], wts['ffn'], wts['wb0'], wts['wb1'], wts['w_out'], wts['w_router'], wts['b_router']]
    return pl.pallas_call(
        _merge_kernel,
        name="merge",
        grid=(t // tm,),
        in_specs=[row(D_MODEL), row(ATT_W), row(HG_WIDTH), row(HG_WIDTH), row(HG_WIDTH), row(D_MODEL),
                  row(D_MODEL)] + [_const_spec(c.shape) for c in consts],
        out_specs=[row(D_MODEL), row(D_MODEL), row(LANES)],
        out_shape=[jax.ShapeDtypeStruct((t, D_MODEL), F32), jax.ShapeDtypeStruct((t, D_MODEL), BF16),
                   jax.ShapeDtypeStruct((t, LANES), F32)],
        compiler_params=_params(("parallel",)),
    )(x2d, oa, of, ob, hg, ga, gb, *consts)


def _moe_kernel(x1_ref, h2_ref, comb_ref, wg_ref, wu_ref, wd_ref, fn_ref, y_ref,
                acc_ref, pos_ref, cw_ref, tri_ref, hh_ref, nblk_ref, *, blk):
    tile = pl.program_id(0)
    grp = pl.program_id(1)
    tm = h2_ref.shape[0]

    @pl.when((tile == 0) & (grp == 0))
    def _():
        r = lax.broadcasted_iota(jnp.int32, (tm, tm), 0)
        c = lax.broadcasted_iota(jnp.int32, (tm, tm), 1)
        tri_ref[...] = jnp.where(r < c, 1.0, 0.0).astype(BF16)

    @pl.when(grp == 0)
    def _():
        acc_ref[...] = jnp.zeros_like(acc_ref)
        comb = comb_ref[...]
        hi = comb.astype(BF16)
        cw_ref[:, :LANES] = hi
        cw_ref[:, LANES:] = (comb - hi.astype(F32)).astype(BF16)
        gid = comb.T[N_EXPERTS:N_EXPERTS + 1, :].astype(jnp.int32)
        g8 = lax.broadcasted_iota(jnp.int32, (8, tm), 0)
        onehot = g8 == gid
        ones = jnp.where(onehot, 1.0, 0.0)
        rank = _dot(ones.astype(BF16), tri_ref[...])
        nblk = (jnp.sum(ones, axis=1, keepdims=True).astype(jnp.int32) + (blk - 1)) >> int(math.log2(blk))
        start = jnp.zeros((8, 1), jnp.int32)
        run = jnp.zeros((1, 1), jnp.int32)
        g81 = lax.broadcasted_iota(jnp.int32, (8, 1), 0)
        for k in range(N_GROUPS):
            start = jnp.where(g81 == k, run, start)
            nblk_ref[0, k] = nblk[k, 0]
            nblk_ref[1, k] = run[0, 0]
            run = run + nblk[k:k + 1, :]
        pos = jnp.sum(jnp.where(onehot, rank + (start * blk).astype(F32), 0.0), axis=0, keepdims=True)
        pos_ref[...] = pos.astype(jnp.int32)

    first = nblk_ref[1, grp]

    def block(b, carry):
        rows = (first + b) * blk + lax.broadcasted_iota(jnp.int32, (blk, tm), 0)
        p = jnp.where(rows == pos_ref[...], 1.0, 0.0).astype(BF16)
        xs = _dot(p, h2_ref[...]).astype(BF16)
        cw2 = _dot(p, cw_ref[...])
        cw = cw2[:, :LANES] + cw2[:, LANES:]
        gate = _dot(xs, wg_ref[0])
        up = _dot(xs, wu_ref[0])
        lane = lax.broadcasted_iota(jnp.int32, cw.shape, 1)
        for e in range(EXPERTS_PER_GROUP):
            lo, hi = e * D_EXPERT, (e + 1) * D_EXPERT
            w_e = jnp.sum(jnp.where(lane == grp * EXPERTS_PER_GROUP + e, cw, 0.0), axis=-1, keepdims=True)
            hh_ref[:, lo:hi] = (jax.nn.silu(gate[:, lo:hi]) * up[:, lo:hi] * w_e).astype(BF16)
        yb = _dot(hh_ref[...], wd_ref[0]).astype(BF16)
        acc_ref[...] += _dot_tn(p, yb)
        return carry

    lax.fori_loop(0, nblk_ref[0, grp], block, 0)

    @pl.when(grp == N_GROUPS - 1)
    def _():
        y_ref[...] = _rms(x1_ref[...] + acc_ref[...], fn_ref[...])


def _moe(x1, h2, comb, wts, tm, blk):
    t = x1.shape[0]
    row = lambda w: pl.BlockSpec((tm, w), lambda i, g: (i, 0))
    return pl.pallas_call(
        functools.partial(_moe_kernel, blk=blk),
        name="moe",
        grid=(t // tm, N_GROUPS),
        in_specs=[row(D_MODEL), row(D_MODEL), row(LANES),
                  pl.BlockSpec((1, D_MODEL, GROUP_FF), lambda i, g: (g, 0, 0)),
                  pl.BlockSpec((1, D_MODEL, GROUP_FF), lambda i, g: (g, 0, 0)),
                  pl.BlockSpec((1, GROUP_FF, D_MODEL), lambda i, g: (g, 0, 0)),
                  pl.BlockSpec((1, D_MODEL), lambda i, g: (0, 0))],
        out_specs=row(D_MODEL),
        out_shape=jax.ShapeDtypeStruct((t, D_MODEL), F32),
        scratch_shapes=[pltpu.VMEM((tm, D_MODEL), F32), pltpu.VMEM((1, tm), jnp.int32),
                        pltpu.VMEM((tm, 2 * LANES), BF16), pltpu.VMEM((tm, tm), BF16),
                        pltpu.VMEM((blk, GROUP_FF), BF16), pltpu.SMEM((2, N_GROUPS), jnp.int32)],
        compiler_params=_params(("arbitrary", "arbitrary")),
    )(x1, h2, comb, wts['w_gate_e'], wts['w_up_e'], wts['w_down_e'], wts['fn'])


def _rope_tables(seq_len):
    pos = jnp.arange(seq_len, dtype=F32)
    inv_freq = 1.0 / (ROPE_THETA ** (jnp.arange(0, QK_ROPE, 2, dtype=F32) / QK_ROPE))
    ang = pos[:, None] * inv_freq[None, :]
    cos, sin = jnp.cos(ang), jnp.sin(ang)
    pad = jnp.zeros((seq_len, HEAD_PAD - QK_DIM), F32)
    cos_t = jnp.concatenate([jnp.ones((seq_len, QK_NOPE), F32), cos, cos, pad], axis=-1)
    sin_t = jnp.concatenate([jnp.zeros((seq_len, QK_NOPE), F32), sin, sin, pad], axis=-1)
    return cos_t, sin_t


def _rot_half_cols(w):
    half = QK_ROPE // 2
    return jnp.concatenate([-w[..., half:], w[..., :half]], axis=-1)


def _prep_weights(attn_norm, w_in, q_norm, w_uq, kv_norm, w_ukv, lb_param, hg_norm, w_branch, w_out, ffn_norm,
                  w_group, b_group, w_expert, b_expert, w_gate, w_up, w_down, final_norm):
    w_in = w_in[0]
    o_kpe = Q_LORA + KV_LORA
    o_hg = o_kpe + QK_ROPE
    o_gate = o_hg + 3 * HG_KW + 2 * HG_WIDTH
    w_kpe = w_in[:, o_kpe:o_hg]
    zpad = lambda n: jnp.zeros((D_MODEL, n), F32)
    place = lambda w: jnp.concatenate([zpad(QK_NOPE), w, zpad(HEAD_PAD - QK_DIM)], axis=-1)
    w_kpe2 = jnp.concatenate([place(w_kpe), place(_rot_half_cols(w_kpe))], axis=-1)

    uq = w_uq[0]
    zq = lambda n: jnp.zeros((Q_LORA, MLA_HEADS, n), F32)
    uq_full = jnp.concatenate([uq, zq(HEAD_PAD - QK_DIM)], axis=-1)
    uq_rot = jnp.concatenate([zq(QK_NOPE), _rot_half_cols(uq[..., QK_NOPE:]), zq(HEAD_PAD - QK_DIM)], axis=-1)
    w_uq2 = jnp.concatenate([uq_full.reshape(Q_LORA, ATT_W), uq_rot.reshape(Q_LORA, ATT_W)], axis=-1)

    ukv = w_ukv[0]
    zk = lambda n: jnp.zeros((KV_LORA, MLA_HEADS, n), F32)
    uk = jnp.concatenate([ukv[..., :QK_NOPE], zk(HEAD_PAD - QK_NOPE)], axis=-1)
    uv = jnp.concatenate([ukv[..., QK_NOPE:], zk(HEAD_PAD - V_DIM)], axis=-1)
    w_uk = uk.reshape(KV_LORA, ATT_W)
    w_uvt = uv.reshape(KV_LORA, ATT_W).T

    wb0 = w_branch[0, 0].reshape(MLA_HEADS, V_DIM, D_MODEL)
    wb0 = jnp.concatenate([wb0, jnp.zeros((MLA_HEADS, HEAD_PAD - V_DIM, D_MODEL), F32)], axis=1)

    w_router = jnp.concatenate([w_expert[0], w_group[0],
                                jnp.zeros((D_MODEL, LANES - N_EXPERTS - N_GROUPS), F32)], axis=-1)
    b_router = jnp.concatenate([b_expert[0], b_group[0], jnp.zeros((LANES - N_EXPERTS - N_GROUPS,), F32)])
    w_router_hi = w_router.astype(BF16)
    w_router_lo = (w_router - w_router_hi.astype(F32)).astype(BF16)
    w_router = jnp.concatenate([w_router_hi, w_router_lo], axis=-1)

    ff = lambda w: jnp.transpose(w[0], (0, 2, 1, 3)).reshape(N_GROUPS, D_MODEL, GROUP_FF).astype(BF16)
    lbs = lb_param.astype(F32)
    return {
        'an': attn_norm[0][None], 'qn': q_norm[0][None], 'kvn': kv_norm[0][None],
        'w_lat': w_in[:, :o_kpe].astype(BF16), 'w_kpe': w_kpe2.astype(BF16),
        'w_hg': w_in[:, o_hg:o_gate].astype(BF16), 'w_gate': w_in[:, o_gate:].astype(BF16),
        'w_uq': w_uq2.astype(BF16), 'w_uk': w_uk.astype(BF16), 'w_uvt': w_uvt.astype(BF16), 'lbs': lbs,
        'hgn': hg_norm[0][None], 'ffn': ffn_norm[0][None],
        'wb0': wb0.reshape(ATT_W, D_MODEL).astype(BF16), 'wb1': w_branch[0, 1].astype(BF16),
        'w_out': w_out[0].astype(BF16), 'w_router': w_router, 'b_router': b_router[None],
        'w_gate_e': ff(w_gate), 'w_up_e': ff(w_up),
        'w_down_e': w_down[0].reshape(N_GROUPS, GROUP_FF, D_MODEL).astype(BF16), 'fn': final_norm[None],
    }


TOKEN_TILE = 512
ATT_TQ = 1024
ATT_TK = 512
ATT_UNROLL = 4
GLA_ROWS = 1024
GLA_CHUNK = 128
MOE_TILE = 1024
MOE_BLOCK = 128


def _tile(n, pref):
    t = min(n, pref)
    assert n % t == 0
    return t


def _encoder(x, wts):
    b, l, d = x.shape
    t = b * l
    x2d = x.reshape(t, d)
    tm = _tile(l, TOKEN_TILE)
    assert tm == _tile(l, ATT_TK)
    q, k, vt, hq, hi, fzf, fzb, hg, ga, gb = _in_proj(x2d, l, _rope_tables(l), wts, tm)
    r3 = lambda a: a.reshape(b, l, a.shape[-1])
    oa = _attention(r3(q), r3(k), vt, _tile(l, ATT_TQ), tm)
    of, ob = _gla(r3(hq), r3(hi), r3(fzf), r3(fzb), wts['lbs'], _tile(l, GLA_ROWS), GLA_CHUNK)
    x1, h2, comb = _merge(x2d, oa.reshape(t, ATT_W), of.reshape(t, HG_WIDTH), ob.reshape(t, HG_WIDTH),
                          hg, ga, gb, wts, tm)
    y = _moe(x1, h2, comb, wts, _tile(l, MOE_TILE), MOE_BLOCK)
    return y.reshape(b, l, d)


def kernel(x_prompt, x_sample, attn_norm, w_in, q_norm, w_uq, kv_norm, w_ukv, lb_param, hg_norm, w_branch, w_out,
           ffn_norm, w_group, b_group, w_expert, b_expert, w_gate, w_up, w_down, final_norm):
    wts = _prep_weights(attn_norm, w_in, q_norm, w_uq, kv_norm, w_ukv, lb_param, hg_norm, w_branch, w_out,
                        ffn_norm, w_group, b_group, w_expert, b_expert, w_gate, w_up, w_down, final_norm)
    return (_encoder(x_prompt, wts), _encoder(x_sample, wts))
```
